```python
import math
import jax, jax.numpy as jnp
from jax import lax
import numpy as np

D_MODEL = 1024
BATCH = 2
SEQ = 16384
DEPTH = 1
DEC_BATCH = 8
DEC_SEQ = 64
PAST_LEN = 2048

CHUNK = 64
Q_BLOCK = 128
RMS_EPS = 1e-6
NEG_INF = -1e30
DA_HEADS = 4
DA_HEAD_DIM = 64
DA_V_DIM = 2 * DA_HEAD_DIM
DA_QK = DA_HEADS * 2 * DA_HEAD_DIM
DA_V = DA_HEADS * DA_V_DIM
GDN_HEADS = 4
GDN_HEAD_DIM = 128
GDN_K = GDN_HEADS * GDN_HEAD_DIM
GDN_V = GDN_HEADS * GDN_HEAD_DIM
GDN_CONV = 4
GDN_CONV_CH = 2 * GDN_K + GDN_V
MIX_WIDTH = DA_V + GDN_V
OFF_KA = DA_QK
OFF_VA = 2 * DA_QK
OFF_CONV = 2 * DA_QK + DA_V
OFF_Z = OFF_CONV + GDN_CONV_CH
OFF_B = OFF_Z + GDN_V
OFF_A = OFF_B + GDN_HEADS
IN_COLS = OFF_A + GDN_HEADS
N_EXPERTS = 32
TOP_K = 4
D_FF = D_MODEL
SWIGLU_LIMIT = 7.0
SWIGLU_ALPHA = 1.702

kernel_name = 'hymba_diffattn_gdn_moe_stream_step'


def rmsnorm(x, g):
    xf = x.astype(jnp.float32)
    y = xf * lax.rsqrt(jnp.mean(xf * xf, axis=-1, keepdims=True) + RMS_EPS)
    return (y * g.astype(jnp.float32)).astype(x.dtype)


def l2norm(x):
    xf = x.astype(jnp.float32)
    return (xf * lax.rsqrt(jnp.sum(xf * xf, axis=-1, keepdims=True) + 1e-6)).astype(x.dtype)


def _diff_attn_block(q, k, v, q_pos, k_pos, lam):
    s = jnp.einsum('bqhmd,bkhmd->bhmqk', q, k, preferred_element_type=jnp.float32) * (DA_HEAD_DIM ** -0.5)
    visible = (q_pos[:, None] // CHUNK) >= (k_pos[None, :] // CHUNK)
    p = jax.nn.softmax(jnp.where(visible, s, NEG_INF), axis=-1)
    a = p[:, :, 0] - lam * p[:, :, 1]
    return jnp.einsum('bhqk,bkhd->bqhd', a.astype(v.dtype), v)


def diff_attention(q, k, v, q_pos, k_pos, lam):
    B, L = q.shape[0], q.shape[1]
    if L <= Q_BLOCK:
        return _diff_attn_block(q, k, v, q_pos, k_pos, lam)
    nb = L // Q_BLOCK
    qb = jnp.moveaxis(q.reshape((B, nb, Q_BLOCK) + q.shape[2:]), 1, 0)
    pb = q_pos.reshape(nb, Q_BLOCK)
    ob = lax.map(lambda a: _diff_attn_block(a[0], k, v, a[1], k_pos, lam), (qb, pb))
    return jnp.moveaxis(ob, 0, 1).reshape(B, L, DA_HEADS, DA_V_DIM)


def causal_conv(u, buf, w):
    up = jnp.concatenate([buf.astype(u.dtype), u], axis=1)
    y = lax.conv_general_dilated(up, w[:, None, :].astype(u.dtype), (1,), 'VALID',
                                 dimension_numbers=('NWC', 'WIO', 'NWC'),
                                 feature_group_count=u.shape[-1])
    return y, up[:, up.shape[1] - (GDN_CONV - 1):]


def gated_delta_chunked(q, k, v, g, beta, S0):
    B, L, H, DK = q.shape
    DV = v.shape[-1]
    C = CHUNK if L % CHUNK == 0 else L
    N = L // C
    f32 = jnp.float32

    def blk(t):
        t = t.astype(f32).reshape((B, N, C, H) + t.shape[3:])
        return jnp.moveaxis(t, 3, 1)

    q, k, v, g, beta = blk(q), blk(k), blk(v), blk(g), blk(beta)
    G = jnp.cumsum(g, axis=-1)
    causal = jnp.tril(jnp.ones((C, C), bool))
    strict = jnp.tril(jnp.ones((C, C), bool), -1)
    diff = G[..., :, None] - G[..., None, :]
    decay = jnp.where(causal, jnp.exp(jnp.where(causal, diff, 0.0)), 0.0)
    kb = k * beta[..., None]
    A = jnp.eye(C, dtype=f32) + jnp.where(strict, jnp.einsum('bhncd,bhnjd->bhncj', kb, k) * decay, 0.0)
    rhs = jnp.concatenate([v * beta[..., None], kb * jnp.exp(G)[..., None]], axis=-1)
    sol = lax.linalg.triangular_solve(A, rhs, left_side=True, lower=True, unit_diagonal=True)
    u, w = sol[..., :DV], sol[..., DV:]
    qk = jnp.einsum('bhncd,bhnjd->bhncj', q, k) * decay
    qg = q * jnp.exp(G)[..., None]
    kg = k * jnp.exp(G[..., -1:] - G)[..., None]
    gl = jnp.exp(G[..., -1])

    def step(S, xs):
        u_c, w_c, qk_c, qg_c, kg_c, gl_c = xs
        v_new = u_c - jnp.einsum('bhcd,bhde->bhce', w_c, S)
        o = jnp.einsum('bhcd,bhde->bhce', qg_c, S) + jnp.einsum('bhcj,bhje->bhce', qk_c, v_new)
        S = S * gl_c[..., None, None] + jnp.einsum('bhcd,bhce->bhde', kg_c, v_new)
        return S, o

    xs = tuple(jnp.moveaxis(t, 2, 0) for t in (u, w, qk, qg, kg, gl))
    S, o = lax.scan(step, S0.astype(f32), xs)
    o = jnp.transpose(o, (1, 0, 3, 2, 4)).reshape(B, L, H, DV)
    return o, S


def moe(h, w_router, b_router, w_gate, b_gate, w_up, b_up, w_down, b_down):
    B, L, D = h.shape
    t = h.reshape(B * L, D)
    logits = (t @ w_router).astype(jnp.float32) + b_router.astype(jnp.float32)
    top_v, top_i = lax.top_k(logits, TOP_K)
    gates = jax.nn.softmax(top_v, axis=-1)
    comb = jnp.sum(jax.nn.one_hot(top_i, N_EXPERTS, dtype=jnp.float32) * gates[..., None], axis=1)
    out = jnp.zeros((B * L, D), jnp.float32)
    for e in range(N_EXPERTS):
        gt = jnp.minimum(t @ w_gate[e] + b_gate[e], SWIGLU_LIMIT)
        up = jnp.clip(t @ w_up[e] + b_up[e], -SWIGLU_LIMIT, SWIGLU_LIMIT)
        act = (up + 1.0) * gt * jax.nn.sigmoid(SWIGLU_ALPHA * gt)
        out = out + comb[:, e:e + 1] * (act @ w_down[e] + b_down[e]).astype(jnp.float32)
    return out.astype(h.dtype).reshape(B, L, D)


def hybrid_layer(x, k_past, v_past, S0, conv0, lam_init, norm1_g, w_in, lambda_q1, lambda_k1, lambda_q2,
                 lambda_k2, da_subln_g, conv_w, a_log, dt_bias, gdn_norm_g, w_out, norm2_g, w_router,
                 b_router, w_gate, b_gate, w_up, b_up, w_down, b_down):
    B, L, _ = x.shape
    f32 = jnp.float32
    h = rmsnorm(x, norm1_g)
    p = h @ w_in
    qa = p[..., :OFF_KA].reshape(B, L, DA_HEADS, 2, DA_HEAD_DIM)
    ka = p[..., OFF_KA:OFF_VA].reshape(B, L, DA_HEADS, 2, DA_HEAD_DIM)
    va = p[..., OFF_VA:OFF_CONV].reshape(B, L, DA_HEADS, DA_V_DIM)
    if k_past is None:
        pos0, k_all, v_all = 0, ka, va
    else:
        pos0 = k_past.shape[1]
        k_all = jnp.concatenate([k_past.astype(ka.dtype), ka], axis=1)
        v_all = jnp.concatenate([v_past.astype(va.dtype), va], axis=1)
    q_pos = pos0 + jnp.arange(L)
    k_pos = jnp.arange(pos0 + L)
    lam = (jnp.exp(jnp.sum(lambda_q1.astype(f32) * lambda_k1.astype(f32)))
           - jnp.exp(jnp.sum(lambda_q2.astype(f32) * lambda_k2.astype(f32))) + lam_init)
    oa = diff_attention(qa, k_all, v_all, q_pos, k_pos, lam)
    oa = (rmsnorm(oa, da_subln_g) * (1.0 - lam_init)).reshape(B, L, DA_V)
    cqkv, conv_new = causal_conv(p[..., OFF_CONV:OFF_Z], conv0, conv_w)
    cqkv = jax.nn.silu(cqkv)
    gq = l2norm(cqkv[..., :GDN_K].reshape(B, L, GDN_HEADS, GDN_HEAD_DIM)) * (GDN_HEAD_DIM ** -0.5)
    gk = l2norm(cqkv[..., GDN_K:2 * GDN_K].reshape(B, L, GDN_HEADS, GDN_HEAD_DIM))
    gv = cqkv[..., 2 * GDN_K:].reshape(B, L, GDN_HEADS, GDN_HEAD_DIM)
    z = p[..., OFF_Z:OFF_B].reshape(B, L, GDN_HEADS, GDN_HEAD_DIM)
    beta = jax.nn.sigmoid(p[..., OFF_B:OFF_A].astype(f32))
    g = -jnp.exp(a_log.astype(f32)) * jax.nn.softplus(p[..., OFF_A:].astype(f32) + dt_bias.astype(f32))
    og, S_new = gated_delta_chunked(gq, gk, gv, g, beta, S0)
    og = (rmsnorm(og.astype(x.dtype), gdn_norm_g) * jax.nn.silu(z)).reshape(B, L, GDN_V)
    x = x + jnp.concatenate([oa, og], axis=-1) @ w_out
    x = x + moe(rmsnorm(x, norm2_g), w_router, b_router, w_gate, b_gate, w_up, b_up, w_down, b_down)
    return x, ka, va, S_new.astype(S0.dtype), conv_new


def setup_inputs(seed: int = 0) -> dict:
    key = jax.random.key(seed)
    ks = jax.random.split(key, 28)
    f32 = jnp.float32

    def nrm(k, shape, scale):
        return scale * jax.random.normal(k, shape, f32)

    dt = jnp.exp(jax.random.uniform(ks[0], (DEPTH, GDN_HEADS), f32, minval=math.log(1e-3), maxval=math.log(1e-1)))
    return {
        'x_prompt': nrm(ks[1], (BATCH, SEQ, D_MODEL), 1.0),
        'x_sample': nrm(ks[2], (DEC_BATCH, DEC_SEQ, D_MODEL), 1.0),
        'cache_k': nrm(ks[3], (DEPTH, DEC_BATCH, PAST_LEN, DA_HEADS, 2, DA_HEAD_DIM), 1.0),
        'cache_v': nrm(ks[4], (DEPTH, DEC_BATCH, PAST_LEN, DA_HEADS, DA_V_DIM), 1.0),
        'state_ssm': nrm(ks[5], (DEPTH, DEC_BATCH, GDN_HEADS, GDN_HEAD_DIM, GDN_HEAD_DIM), 0.1),
        'state_conv': nrm(ks[6], (DEPTH, DEC_BATCH, GDN_CONV - 1, GDN_CONV_CH), 1.0),
        'norm1_g': 1.0 + nrm(ks[7], (DEPTH, D_MODEL), 0.02),
        'w_in': nrm(ks[8], (DEPTH, D_MODEL, IN_COLS), D_MODEL ** -0.5),
        'lambda_q1': nrm(ks[9], (DEPTH, DA_HEAD_DIM), 0.1),
        'lambda_k1': nrm(ks[10], (DEPTH, DA_HEAD_DIM), 0.1),
        'lambda_q2': nrm(ks[11], (DEPTH, DA_HEAD_DIM), 0.1),
        'lambda_k2': nrm(ks[12], (DEPTH, DA_HEAD_DIM), 0.1),
        'da_subln_g': 1.0 + nrm(ks[13], (DEPTH, DA_V_DIM), 0.02),
        'conv_w': nrm(ks[14], (DEPTH, GDN_CONV, GDN_CONV_CH), GDN_CONV ** -0.5),
        'a_log': jnp.log(jax.random.uniform(ks[15], (DEPTH, GDN_HEADS), f32, minval=1.0, maxval=16.0)),
        'dt_bias': dt + jnp.log(-jnp.expm1(-dt)),
        'gdn_norm_g': 1.0 + nrm(ks[16], (DEPTH, GDN_HEAD_DIM), 0.02),
        'w_out': nrm(ks[17], (DEPTH, MIX_WIDTH, D_MODEL), MIX_WIDTH ** -0.5),
        'norm2_g': 1.0 + nrm(ks[18], (DEPTH, D_MODEL), 0.02),
        'w_router': nrm(ks[19], (DEPTH, D_MODEL, N_EXPERTS), D_MODEL ** -0.5),
        'b_router': nrm(ks[20], (DEPTH, N_EXPERTS), 0.01),
        'w_gate': nrm(ks[21], (DEPTH, N_EXPERTS, D_MODEL, D_FF), D_MODEL ** -0.5),
        'b_gate': nrm(ks[22], (DEPTH, N_EXPERTS, D_FF), 0.01),
        'w_up': nrm(ks[23], (DEPTH, N_EXPERTS, D_MODEL, D_FF), D_MODEL ** -0.5),
        'b_up': nrm(ks[24], (DEPTH, N_EXPERTS, D_FF), 0.01),
        'w_down': nrm(ks[25], (DEPTH, N_EXPERTS, D_FF, D_MODEL), D_FF ** -0.5),
        'b_down': nrm(ks[26], (DEPTH, N_EXPERTS, D_MODEL), 0.01),
        'final_g': 1.0 + nrm(ks[27], (D_MODEL,), 0.02),
    }


def reference(x_prompt, x_sample, cache_k, cache_v, state_ssm, state_conv, norm1_g, w_in, lambda_q1,
              lambda_k1, lambda_q2, lambda_k2, da_subln_g, conv_w, a_log, dt_bias, gdn_norm_g, w_out,
              norm2_g, w_router, b_router, w_gate, b_gate, w_up, b_up, w_down, b_down, final_g):
    xp, xs = x_prompt, x_sample
    bp = x_prompt.shape[0]
    kp, vp, sp, cp, ksl, vsl, ssl, csl = [], [], [], [], [], [], [], []
    for l in range(DEPTH):
        lam_init = 0.8 - 0.6 * math.exp(-0.3 * l)
        lw = (norm1_g[l], w_in[l], lambda_q1[l], lambda_k1[l], lambda_q2[l], lambda_k2[l], da_subln_g[l],
              conv_w[l], a_log[l], dt_bias[l], gdn_norm_g[l], w_out[l], norm2_g[l], w_router[l], b_router[l],
              w_gate[l], b_gate[l], w_up[l], b_up[l], w_down[l], b_down[l])
        s0 = jnp.zeros((bp, GDN_HEADS, GDN_HEAD_DIM, GDN_HEAD_DIM), x_prompt.dtype)
        c0 = jnp.zeros((bp, GDN_CONV - 1, GDN_CONV_CH), x_prompt.dtype)
        xp, k1, v1, s1, c1 = hybrid_layer(xp, None, None, s0, c0, lam_init, *lw)
        xs, k2, v2, s2, c2 = hybrid_layer(xs, cache_k[l], cache_v[l], state_ssm[l], state_conv[l], lam_init, *lw)
        kp.append(k1); vp.append(v1); sp.append(s1); cp.append(c1)
        ksl.append(k2); vsl.append(v2); ssl.append(s2); csl.append(c2)
    y_prompt = rmsnorm(xp, final_g)
    y_sample = rmsnorm(xs, final_g)
    k_prompt, v_prompt, ssm_prompt, conv_prompt = jnp.stack(kp), jnp.stack(vp), jnp.stack(sp), jnp.stack(cp)
    k_sample, v_sample, ssm_sample, conv_sample = jnp.stack(ksl), jnp.stack(vsl), jnp.stack(ssl), jnp.stack(csl)
    return (y_prompt, y_sample, k_prompt, v_prompt, ssm_prompt, conv_prompt, k_sample, v_sample, ssm_sample, conv_sample)
```

```python
import functools
import math

import jax
import jax.numpy as jnp
from jax import lax
from jax.experimental import pallas as pl
from jax.experimental.pallas import tpu as pltpu

F32 = jnp.float32
BF16 = jnp.bfloat16

D_MODEL = 1024
CHUNK = 64
RMS_EPS = 1e-6
NEG_INF = -1e30
DA_HEADS = 4
DA_HEAD_DIM = 64
DA_V_DIM = 128
DA_QK = 512
DA_V = 512
GDN_HEADS = 4
GDN_HEAD_DIM = 128
GDN_K = 512
GDN_V = 512
GDN_CONV = 4
GDN_CONV_CH = 1536
OFF_KA = DA_QK
OFF_VA = 2 * DA_QK
OFF_CONV = 2 * DA_QK + DA_V
OFF_Z = OFF_CONV + GDN_CONV_CH
OFF_B = OFF_Z + GDN_V
N_EXPERTS = 32
TOP_K = 4
D_FF = 1024
SWIGLU_LIMIT = 7.0
SWIGLU_ALPHA = 1.702
LANES = 128
VMEM_LIMIT = 56 * 1024 * 1024


def _split_bf16(a):
    hi = a.astype(BF16)
    lo = (a - hi.astype(F32)).astype(BF16)
    return hi, lo


def _dot(a, b, dims=(((1,), (0,)), ((), ()))):
    return lax.dot_general(a, b, dims, preferred_element_type=F32)


def _dot3(a, b, dims=(((1,), (0,)), ((), ()))):
    ah, al = _split_bf16(a)
    bh, bl = _split_bf16(b)
    return _dot(ah, bh, dims) + (_dot(ah, bl, dims) + _dot(al, bh, dims))


_NT = (((1,), (1,)), ((), ()))


def _rms(x, g):
    return x * lax.rsqrt(jnp.mean(x * x, axis=-1, keepdims=True) + RMS_EPS) * g


def _inproj_kernel(x_ref, g_ref, w_ref, wg_ref, qb_ref, k_ref, kb_ref, v_ref, vb_ref, cv_ref, z_ref, gt_ref):
    h = _rms(x_ref[...], g_ref[...])
    hb = h.astype(BF16)
    q = _dot(hb, w_ref[:, 0:OFF_KA])
    qb_ref[...] = (q * (DA_HEAD_DIM ** -0.5)).astype(BF16)
    k = _dot(hb, w_ref[:, OFF_KA:OFF_VA])
    k_ref[...] = k
    kb_ref[...] = k.astype(BF16)
    v = _dot(hb, w_ref[:, OFF_VA:OFF_CONV])
    v_ref[...] = v
    vb_ref[...] = v.astype(BF16)
    for c in range(GDN_CONV_CH // 512):
        cv_ref[:, c * 512:(c + 1) * 512] = _dot(hb, w_ref[:, OFF_CONV + c * 512:OFF_CONV + (c + 1) * 512])
    z_ref[...] = _dot(hb, w_ref[:, OFF_Z:OFF_B])
    gt_ref[...] = _dot3(h, wg_ref[...])


def _in_proj(x, norm1_g, w_main, w_gates, tm):
    T = x.shape[0]
    row = lambda n: pl.BlockSpec((tm, n), lambda i: (i, 0))
    full = lambda a: pl.BlockSpec(a.shape, lambda i: (0,) * a.ndim)
    out_shape = (
        jax.ShapeDtypeStruct((T, DA_QK), BF16),
        jax.ShapeDtypeStruct((T, DA_QK), F32),
        jax.ShapeDtypeStruct((T, DA_QK), BF16),
        jax.ShapeDtypeStruct((T, DA_V), F32),
        jax.ShapeDtypeStruct((T, DA_V), BF16),
        jax.ShapeDtypeStruct((T, GDN_CONV_CH), F32),
        jax.ShapeDtypeStruct((T, GDN_V), F32),
        jax.ShapeDtypeStruct((T, LANES), F32),
    )
    return pl.pallas_call(
        _inproj_kernel,
        grid=(T // tm,),
        in_specs=[row(D_MODEL), full(norm1_g), full(w_main), full(w_gates)],
        out_specs=tuple(row(s.shape[1]) for s in out_shape),
        out_shape=out_shape,
        compiler_params=pltpu.CompilerParams(dimension_semantics=("arbitrary",), vmem_limit_bytes=VMEM_LIMIT),
        name="in_proj",
    )(x, norm1_g, w_main, w_gates)


def _lam_from(lamv_ref, lam_init):
    lv = lamv_ref[...]
    s1 = jnp.sum(lv[0:1, :] * lv[1:2, :], axis=-1, keepdims=True)
    s2 = jnp.sum(lv[2:3, :] * lv[3:4, :], axis=-1, keepdims=True)
    return jnp.exp(s1) - jnp.exp(s2) + lam_init


def _stack_maps(q):
    lane = lax.broadcasted_iota(jnp.int32, q.shape, 1)
    zero = jnp.zeros_like(q)
    return jnp.concatenate([jnp.where(lane < DA_HEAD_DIM, q, zero), jnp.where(lane >= DA_HEAD_DIM, q, zero)], axis=0)


def _attn_finish(acc, l, lam, subg, lam_init, tq):
    o = acc[:tq] / l[:tq] - lam * (acc[tq:] / l[tq:])
    return _rms(o, subg) * (1.0 - lam_init)


def _attn_prompt_kernel(lamv_ref, subg_ref, q_ref, k_ref, v_ref, o_ref, qs_sc, m_sc, l_sc, acc_sc, *, tq, lam_init):
    i = pl.program_id(2)
    qs_sc[...] = _stack_maps(q_ref[...])
    m_sc[...] = jnp.full(m_sc.shape, NEG_INF, F32)
    l_sc[...] = jnp.zeros(l_sc.shape, F32)
    acc_sc[...] = jnp.zeros(acc_sc.shape, F32)

    def block(j, masked):
        start = pl.multiple_of(j * tq, tq)
        kb = k_ref[pl.ds(start, tq), :]
        vb = v_ref[pl.ds(start, tq), :]
        s = _dot(qs_sc[...], kb, _NT)
        if masked:
            r = lax.broadcasted_iota(jnp.int32, s.shape, 0)
            c = lax.broadcasted_iota(jnp.int32, s.shape, 1)
            r = jnp.where(r >= tq, r - tq, r)
            s = jnp.where((r // CHUNK) >= (c // CHUNK), s, NEG_INF)
        m_prev = m_sc[...]
        m_new = jnp.maximum(m_prev, jnp.max(s, axis=1, keepdims=True))
        alpha = jnp.exp(m_prev - m_new)
        p = jnp.exp(s - m_new)
        l_sc[...] = alpha * l_sc[...] + jnp.sum(p, axis=1, keepdims=True)
        acc_sc[...] = alpha * acc_sc[...] + _dot(p.astype(BF16), vb)
        m_sc[...] = m_new

    def body(j, carry):
        block(j, False)
        return carry

    lax.fori_loop(0, i, body, 0)
    block(i, True)
    lam = _lam_from(lamv_ref, lam_init)
    o_ref[...] = _attn_finish(acc_sc[...], l_sc[...], lam, subg_ref[...], lam_init, tq).astype(o_ref.dtype)


def _attn_prompt(qb, kb, vb, lamv, subg, lam_init, tq):
    B, L, _ = qb.shape
    kern = functools.partial(_attn_prompt_kernel, tq=tq, lam_init=lam_init)
    return pl.pallas_call(
        kern,
        grid=(B, DA_HEADS, L // tq),
        in_specs=[
            pl.BlockSpec(lamv.shape, lambda b, h, i: (0, 0)),
            pl.BlockSpec(subg.shape, lambda b, h, i: (0, 0)),
            pl.BlockSpec((None, tq, LANES), lambda b, h, i: (b, i, h)),
            pl.BlockSpec((None, L, LANES), lambda b, h, i: (b, 0, h)),
            pl.BlockSpec((None, L, LANES), lambda b, h, i: (b, 0, h)),
        ],
        out_specs=pl.BlockSpec((None, tq, LANES), lambda b, h, i: (b, i, h)),
        out_shape=jax.ShapeDtypeStruct((B, L, DA_V), BF16),
        scratch_shapes=[
            pltpu.VMEM((2 * tq, LANES), BF16),
            pltpu.VMEM((2 * tq, 1), F32),
            pltpu.VMEM((2 * tq, 1), F32),
            pltpu.VMEM((2 * tq, LANES), F32),
        ],
        compiler_params=pltpu.CompilerParams(
            dimension_semantics=("arbitrary", "arbitrary", "arbitrary"), vmem_limit_bytes=VMEM_LIMIT),
        name="attn_prompt",
    )(lamv, subg, qb, kb, vb)


def _attn_sample_kernel(lamv_ref, subg_ref, q_ref, kp_ref, vp_ref, kn_ref, vn_ref, o_ref, *, lam_init):
    tq = q_ref.shape[0]
    qs = _stack_maps(q_ref[...])
    sp = _dot(qs, kp_ref[...].astype(BF16), _NT)
    sn = _dot(qs, kn_ref[...], _NT)
    m = jnp.maximum(jnp.max(sp, axis=1, keepdims=True), jnp.max(sn, axis=1, keepdims=True))
    pp = jnp.exp(sp - m)
    pn = jnp.exp(sn - m)
    l = jnp.sum(pp, axis=1, keepdims=True) + jnp.sum(pn, axis=1, keepdims=True)
    acc = _dot(pp.astype(BF16), vp_ref[...].astype(BF16)) + _dot(pn.astype(BF16), vn_ref[...])
    lam = _lam_from(lamv_ref, lam_init)
    o_ref[...] = _attn_finish(acc, l, lam, subg_ref[...], lam_init, tq).astype(o_ref.dtype)


def _attn_sample(qb, kb, vb, cache_k, cache_v, lamv, subg, lam_init):
    B, L, _ = qb.shape
    P = cache_k.shape[1]
    kern = functools.partial(_attn_sample_kernel, lam_init=lam_init)
    new = pl.BlockSpec((None, L, LANES), lambda b, h: (b, 0, h))
    past = pl.BlockSpec((None, P, LANES), lambda b, h: (b, 0, h))
    return pl.pallas_call(
        kern,
        grid=(B, DA_HEADS),
        in_specs=[
            pl.BlockSpec(lamv.shape, lambda b, h: (0, 0)),
            pl.BlockSpec(subg.shape, lambda b, h: (0, 0)),
            new, past, past, new, new,
        ],
        out_specs=new,
        out_shape=jax.ShapeDtypeStruct((B, L, DA_V), BF16),
        compiler_params=pltpu.CompilerParams(dimension_semantics=("arbitrary", "arbitrary")),
        name="attn_sample",
    )(lamv, subg, qb, cache_k, cache_v, kb, vb)


_PAD = 8


def _gdn_kernel(u_ref, gt_ref, z_ref, c0_ref, s0_ref, cw_ref, alog_ref, dtb_ref, gn_ref,
                og_ref, sout_ref, ubuf, s_sc):
    c = pl.program_id(1)
    nc = pl.num_programs(1)
    C = CHUNK

    @pl.when(c == 0)
    def _():
        ubuf[0:_PAD, :] = c0_ref[...]
        s_sc[...] = s0_ref[...]

    u = u_ref[...]
    ubuf[_PAD:_PAD + C, :] = u
    cw = cw_ref[...]
    y = u * cw[3:4, :]
    for j in range(1, GDN_CONV):
        y = y + ubuf[_PAD - j:_PAD - j + C, :] * cw[3 - j:4 - j, :]
    ubuf[0:_PAD, :] = u[C - _PAD:, :]
    cq = y * jax.nn.sigmoid(y)

    gt = gt_ref[...]
    beta_all = jax.nn.sigmoid(gt)
    xg = gt + dtb_ref[...]
    softplus = jnp.maximum(xg, 0.0) + jnp.log1p(jnp.exp(-jnp.abs(xg)))
    g_all = -jnp.exp(alog_ref[...]) * softplus
    row = lax.broadcasted_iota(jnp.int32, (C, C), 0)
    col = lax.broadcasted_iota(jnp.int32, (C, C), 1)
    causal = row >= col
    strict = row > col
    tril = jnp.where(causal, 1.0, 0.0).astype(F32)
    ghi = g_all.astype(BF16)
    r1 = g_all - ghi.astype(F32)
    gmid = r1.astype(BF16)
    glo = (r1 - gmid.astype(F32)).astype(BF16)
    trilb = tril.astype(BF16)
    G_all = _dot(trilb, ghi) + (_dot(trilb, gmid) + _dot(trilb, glo))
    G_t = jnp.transpose(G_all)
    eye = jnp.where(row == col, 1.0, 0.0).astype(F32)
    gn = gn_ref[...]

    for h in range(GDN_HEADS):
        sl = slice(h * GDN_HEAD_DIM, (h + 1) * GDN_HEAD_DIM)
        q = cq[:, sl]
        k = cq[:, GDN_K + h * GDN_HEAD_DIM:GDN_K + (h + 1) * GDN_HEAD_DIM]
        v = cq[:, 2 * GDN_K + h * GDN_HEAD_DIM:2 * GDN_K + (h + 1) * GDN_HEAD_DIM]
        q = q * lax.rsqrt(jnp.sum(q * q, axis=-1, keepdims=True) + 1e-6) * (GDN_HEAD_DIM ** -0.5)
        k = k * lax.rsqrt(jnp.sum(k * k, axis=-1, keepdims=True) + 1e-6)
        beta = beta_all[:, h:h + 1]
        Gc = G_all[:, GDN_HEADS + h:GDN_HEADS + h + 1]
        Gr = G_t[GDN_HEADS + h:GDN_HEADS + h + 1, :]
        Gl = Gc[C - 1:C, :]
        decay = jnp.where(causal, jnp.exp(jnp.where(causal, Gc - Gr, 0.0)), 0.0)
        eG = jnp.exp(Gc)
        kb = k * beta
        nm = jnp.where(strict, _dot3(kb, k, _NT) * decay, 0.0)
        inv = eye - nm
        pw = nm
        for _ in range(5):
            pw = _dot3(pw, pw)
            inv = inv + _dot3(inv, pw)
        rhs = jnp.concatenate([v * beta, kb * eG], axis=1)
        sol = _dot3(inv, rhs)
        uu = sol[:, :GDN_HEAD_DIM]
        ww = sol[:, GDN_HEAD_DIM:]
        qk = _dot3(q, k, _NT) * decay
        qg = q * eG
        kg = k * jnp.exp(Gl - Gc)
        S = s_sc[h]
        v_new = uu - _dot3(ww, S)
        o = _dot3(qg, S) + _dot3(qk, v_new)
        s_sc[h] = S * jnp.exp(Gl) + _dot3(jnp.transpose(kg), v_new)
        zz = z_ref[:, sl]
        og_ref[:, sl] = (_rms(o, gn) * (zz * jax.nn.sigmoid(zz))).astype(og_ref.dtype)

    @pl.when(c == nc - 1)
    def _():
        sout_ref[...] = s_sc[...]


def _gdn(conv_in, gates, z, conv0p, s0, conv_w, alog_p, dtb_p, gn):
    B, L, _ = conv_in.shape
    nc = L // CHUNK
    tok = lambda n: pl.BlockSpec((None, CHUNK, n), lambda b, c: (b, c, 0))
    full = lambda a: pl.BlockSpec(a.shape, lambda b, c: (0,) * a.ndim)
    return pl.pallas_call(
        _gdn_kernel,
        grid=(B, nc),
        in_specs=[
            tok(GDN_CONV_CH), tok(LANES), tok(GDN_V),
            pl.BlockSpec((None, _PAD, GDN_CONV_CH), lambda b, c: (b, 0, 0)),
            pl.BlockSpec((None, GDN_HEADS, GDN_HEAD_DIM, GDN_HEAD_DIM), lambda b, c: (b, 0, 0, 0)),
            full(conv_w), full(alog_p), full(dtb_p), full(gn),
        ],
        out_specs=(
            tok(GDN_V),
            pl.BlockSpec((None, GDN_HEADS, GDN_HEAD_DIM, GDN_HEAD_DIM), lambda b, c: (b, 0, 0, 0)),
        ),
        out_shape=(
            jax.ShapeDtypeStruct((B, L, GDN_V), BF16),
            jax.ShapeDtypeStruct((B, GDN_HEADS, GDN_HEAD_DIM, GDN_HEAD_DIM), F32),
        ),
        scratch_shapes=[
            pltpu.VMEM((_PAD + CHUNK, GDN_CONV_CH), F32),
            pltpu.VMEM((GDN_HEADS, GDN_HEAD_DIM, GDN_HEAD_DIM), F32),
        ],
        compiler_params=pltpu.CompilerParams(dimension_semantics=("arbitrary", "arbitrary")),
        name="gdn",
    )(conv_in, gates, z, conv0p, s0, conv_w, alog_p, dtb_p, gn)


def _postmix_kernel(x_ref, oa_ref, og_ref, wo_ref, g2_ref, wr_ref, br_ref, x1_ref, h2_ref, comb_ref):
    mix = _dot(oa_ref[...], wo_ref[0:DA_V, :]) + _dot(og_ref[...], wo_ref[DA_V:, :])
    x1 = x_ref[...] + mix
    x1_ref[...] = x1
    h2 = _rms(x1, g2_ref[...])
    h2_ref[...] = h2.astype(BF16)
    logits = _dot3(h2, wr_ref[...]) + br_ref[...]
    lane = lax.broadcasted_iota(jnp.int32, logits.shape, 1).astype(F32)
    work = logits
    sel = jnp.zeros(logits.shape, jnp.bool_)
    top = jnp.max(work, axis=1, keepdims=True)
    for _ in range(TOP_K):
        mx = jnp.max(work, axis=1, keepdims=True)
        first = jnp.min(jnp.where(work == mx, lane, float(LANES)), axis=1, keepdims=True)
        pick = lane == first
        sel = jnp.logical_or(sel, pick)
        work = jnp.where(pick, -jnp.inf, work)
    e = jnp.where(sel, jnp.exp(logits - top), 0.0)
    comb_ref[...] = e / jnp.sum(e, axis=1, keepdims=True)


def _post_mix(x, oa, og, wo, g2, wr, br, tm):
    T = x.shape[0]
    row = lambda n: pl.BlockSpec((tm, n), lambda i: (i, 0))
    full = lambda a: pl.BlockSpec(a.shape, lambda i: (0,) * a.ndim)
    return pl.pallas_call(
        _postmix_kernel,
        grid=(T // tm,),
        in_specs=[row(D_MODEL), row(DA_V), row(GDN_V), full(wo), full(g2), full(wr), full(br)],
        out_specs=(row(D_MODEL), row(D_MODEL), row(LANES)),
        out_shape=(
            jax.ShapeDtypeStruct((T, D_MODEL), F32),
            jax.ShapeDtypeStruct((T, D_MODEL), BF16),
            jax.ShapeDtypeStruct((T, LANES), F32),
        ),
        compiler_params=pltpu.CompilerParams(dimension_semantics=("arbitrary",), vmem_limit_bytes=VMEM_LIMIT),
        name="post_mix",
    )(x, oa, og, wo, g2, wr, br)


_FF_BLK = 512


def _moe_kernel(h_ref, comb_ref, x1_ref, wg_ref, bg_ref, wu_ref, bu_ref, wd_ref, bd_ref, fg_ref, y_ref, acc_sc):
    e = pl.program_id(1)

    @pl.when(e == 0)
    def _():
        acc_sc[...] = jnp.zeros(acc_sc.shape, F32)

    h = h_ref[...]
    o = bd_ref[...]
    for f in range(D_FF // _FF_BLK):
        fs = slice(f * _FF_BLK, (f + 1) * _FF_BLK)
        gt = jnp.minimum(_dot(h, wg_ref[:, fs]) + bg_ref[:, fs], SWIGLU_LIMIT)
        up = jnp.clip(_dot(h, wu_ref[:, fs]) + bu_ref[:, fs], -SWIGLU_LIMIT, SWIGLU_LIMIT)
        act = (up + 1.0) * gt * jax.nn.sigmoid(SWIGLU_ALPHA * gt)
        o = o + _dot(act.astype(BF16), wd_ref[fs, :])
    comb = comb_ref[...]
    lane = lax.broadcasted_iota(jnp.int32, comb.shape, 1)
    ce = jnp.sum(jnp.where(lane == e, comb, 0.0), axis=1, keepdims=True)
    acc_sc[...] = acc_sc[...] + ce * o

    @pl.when(e == pl.num_programs(1) - 1)
    def _():
        y_ref[...] = _rms(x1_ref[...] + acc_sc[...], fg_ref[...])


def _moe(h2, comb, x1, wg, bg, wu, bu, wd, bd, fg, tm):
    T = h2.shape[0]
    row = lambda n: pl.BlockSpec((tm, n), lambda i, e: (i, 0))
    wspec = pl.BlockSpec((None, D_MODEL, D_FF), lambda i, e: (e, 0, 0))
    bspec = pl.BlockSpec((None, 1, D_FF), lambda i, e: (e, 0, 0))
    return pl.pallas_call(
        _moe_kernel,
        grid=(T // tm, N_EXPERTS),
        in_specs=[row(D_MODEL), row(LANES), row(D_MODEL), wspec, bspec, wspec, bspec, wspec, bspec,
                  pl.BlockSpec(fg.shape, lambda i, e: (0, 0))],
        out_specs=row(D_MODEL),
        out_shape=jax.ShapeDtypeStruct((T, D_MODEL), F32),
        scratch_shapes=[pltpu.VMEM((tm, D_MODEL), F32)],
        compiler_params=pltpu.CompilerParams(
            dimension_semantics=("arbitrary", "arbitrary"), vmem_limit_bytes=VMEM_LIMIT),
        name="moe",
    )(h2, comb, x1, wg, bg, wu, bu, wd, bd, fg)


def _pick(n, prefs):
    for t in prefs:
        if n % t == 0:
            return t
    raise ValueError(f"no tile for {n}")


def _layer(x, k_past, v_past, S0, conv0, lam_init, wts):
    (norm1_g, w_main, w_gates, lamv, subg, conv_w, alog_p, dtb_p, gn, wo, g2, wr, br,
     wg, bg, wu, bu, wd, bd) = wts
    B, L, _ = x.shape
    T = B * L
    xt = x.reshape(T, D_MODEL)
    tm = _pick(T, (256, 128, 64))
    qb, k, kb, v, vb, conv_in, z, gates = _in_proj(xt, norm1_g, w_main, w_gates, tm)
    r3 = lambda a: a.reshape(B, L, a.shape[-1])
    if k_past is None:
        oa = _attn_prompt(r3(qb), r3(kb), r3(vb), lamv, subg, lam_init, _pick(L, (512, 256, 128, 64)))
    else:
        P = k_past.shape[1]
        oa = _attn_sample(r3(qb), r3(kb), r3(vb), k_past.reshape(B, P, DA_QK), v_past.reshape(B, P, DA_V),
                          lamv, subg, lam_init)
    conv_in3 = r3(conv_in)
    conv0p = jnp.concatenate([jnp.zeros((B, _PAD - (GDN_CONV - 1), GDN_CONV_CH), F32), conv0.astype(F32)], axis=1)
    og, s_new = _gdn(conv_in3, r3(gates), r3(z), conv0p, S0.astype(F32), conv_w, alog_p, dtb_p, gn)
    x1, h2, comb = _post_mix(xt, oa.reshape(T, DA_V), og.reshape(T, GDN_V), wo, g2, wr, br, tm)
    return (x1, h2, comb, k.reshape(B, L, DA_HEADS, 2, DA_HEAD_DIM), v.reshape(B, L, DA_HEADS, DA_V_DIM),
            s_new, conv_in3[:, L - (GDN_CONV - 1):, :])


def kernel(x_prompt, x_sample, cache_k, cache_v, state_ssm, state_conv, norm1_g, w_in, lambda_q1, lambda_k1,
           lambda_q2, lambda_k2, da_subln_g, conv_w, a_log, dt_bias, gdn_norm_g, w_out, norm2_g, w_router,
           b_router, w_gate, b_gate, w_up, b_up, w_down, b_down, final_g):
    depth = w_in.shape[0]
    assert depth == 1, "single-layer trunk"
    l = 0
    lam_init = 0.8 - 0.6 * math.exp(-0.3 * l)
    bp = x_prompt.shape[0]
    pad_lanes = lambda a, fill=0.0: jnp.concatenate(
        [a.astype(F32), jnp.full(a.shape[:-1] + (LANES - a.shape[-1],), fill, F32)], axis=-1)
    w_gates = pad_lanes(w_in[l][:, OFF_B:])
    zeros4 = jnp.zeros((GDN_HEADS,), F32)
    alog_p = pad_lanes(jnp.concatenate([zeros4, a_log[l].astype(F32)])[None, :])
    dtb_p = pad_lanes(jnp.concatenate([zeros4, dt_bias[l].astype(F32)])[None, :])
    wts = (
        norm1_g[l][None, :].astype(F32),
        w_in[l][:, :OFF_B].astype(BF16),
        w_gates,
        jnp.stack([lambda_q1[l], lambda_k1[l], lambda_q2[l], lambda_k2[l]]).astype(F32),
        da_subln_g[l][None, :].astype(F32),
        conv_w[l].astype(F32),
        alog_p, dtb_p,
        gdn_norm_g[l][None, :].astype(F32),
        w_out[l].astype(BF16),
        norm2_g[l][None, :].astype(F32),
        pad_lanes(w_router[l]),
        pad_lanes(b_router[l][None, :], NEG_INF),
        w_gate[l].astype(BF16), b_gate[l][:, None, :].astype(F32),
        w_up[l].astype(BF16), b_up[l][:, None, :].astype(F32),
        w_down[l].astype(BF16), b_down[l][:, None, :].astype(F32),
    )
    moe_w = wts[13:]
    fg = final_g[None, :].astype(F32)

    s0 = jnp.zeros((bp, GDN_HEADS, GDN_HEAD_DIM, GDN_HEAD_DIM), x_prompt.dtype)
    c0 = jnp.zeros((bp, GDN_CONV - 1, GDN_CONV_CH), x_prompt.dtype)
    x1p, h2p, combp, k1, v1, s1, c1 = _layer(x_prompt, None, None, s0, c0, lam_init, wts)
    x1s, h2s, combs, k2, v2, s2, c2 = _layer(x_sample, cache_k[l], cache_v[l], state_ssm[l], state_conv[l],
                                             lam_init, wts)
    y_p = _moe(h2p, combp, x1p, *moe_w, fg, _pick(h2p.shape[0], (1024, 512, 256, 128, 64)))
    y_s = _moe(h2s, combs, x1s, *moe_w, fg, _pick(h2s.shape[0], (1024, 512, 256, 128, 64)))
    st = lambda a: a[None]
    return (y_p.reshape(x_prompt.shape), y_s.reshape(x_sample.shape), st(k1), st(v1), st(s1), st(c1),
            st(k2), st(v2), st(s2), st(c2))
```

```python
import functools
import math

import jax
import jax.numpy as jnp
from jax import lax
from jax.experimental import pallas as pl
from jax.experimental.pallas import tpu as pltpu

F32 = jnp.float32
BF16 = jnp.bfloat16

D_MODEL = 1024
CHUNK = 64
RMS_EPS = 1e-6
NEG_INF = -1e30
DA_HEADS = 4
DA_HEAD_DIM = 64
DA_V_DIM = 128
DA_QK = 512
DA_V = 512
GDN_HEADS = 4
GDN_HEAD_DIM = 128
GDN_K = 512
GDN_V = 512
GDN_CONV = 4
GDN_CONV_CH = 1536
OFF_KA = DA_QK
OFF_VA = 2 * DA_QK
OFF_CONV = 2 * DA_QK + DA_V
OFF_Z = OFF_CONV + GDN_CONV_CH
OFF_B = OFF_Z + GDN_V
N_EXPERTS = 32
TOP_K = 4
D_FF = 1024
SWIGLU_LIMIT = 7.0
SWIGLU_ALPHA = 1.702
LANES = 128
VMEM_LIMIT = 56 * 1024 * 1024
LOG2E = 1.4426950408889634


def _split_bf16(a):
    hi = a.astype(BF16)
    lo = (a - hi.astype(F32)).astype(BF16)
    return hi, lo


def _dot(a, b, dims=(((1,), (0,)), ((), ()))):
    return lax.dot_general(a, b, dims, preferred_element_type=F32)


def _dot3(a, b, dims=(((1,), (0,)), ((), ()))):
    ah, al = _split_bf16(a)
    bh, bl = _split_bf16(b)
    return _dot(ah, bh, dims) + (_dot(ah, bl, dims) + _dot(al, bh, dims))


_NT = (((1,), (1,)), ((), ()))


def _rms(x, g):
    return x * lax.rsqrt(jnp.mean(x * x, axis=-1, keepdims=True) + RMS_EPS) * g


_VB = 256
_VR = DA_V_DIM + 16


def _inproj_kernel(x_ref, g_ref, w_ref, wvt_ref, wg_ref,
                   qb_ref, q2_ref, k_ref, kb_ref, v_ref, vb_ref, vt_ref, cv_ref, z_ref, gt_ref):
    h = _rms(x_ref[...], g_ref[...])
    hb = h.astype(BF16)
    q = _dot(hb, w_ref[:, 0:OFF_KA])
    qb_ref[...] = (q * (DA_HEAD_DIM ** -0.5)).astype(BF16)
    q2_ref[...] = (q * (DA_HEAD_DIM ** -0.5 * LOG2E)).astype(BF16)
    k = _dot(hb, w_ref[:, OFF_KA:OFF_VA])
    k_ref[...] = k
    kb_ref[...] = k.astype(BF16)
    v = _dot(hb, w_ref[:, OFF_VA:OFF_CONV])
    v_ref[...] = v
    vb_ref[...] = v.astype(BF16)
    vt = _dot(wvt_ref[...], hb, _NT).astype(BF16)
    sub = lax.broadcasted_iota(jnp.int32, (_VR - DA_V_DIM, _VB), 0)
    ones_row = jnp.where(sub == 0, 1.0, 0.0).astype(BF16)
    for hd in range(DA_HEADS):
        for t in range(vt.shape[1] // _VB):
            vt_ref[hd, t, 0:DA_V_DIM, :] = vt[hd * DA_V_DIM:(hd + 1) * DA_V_DIM, t * _VB:(t + 1) * _VB]
            vt_ref[hd, t, DA_V_DIM:_VR, :] = ones_row
    for c in range(GDN_CONV_CH // 512):
        cv_ref[:, c * 512:(c + 1) * 512] = _dot(hb, w_ref[:, OFF_CONV + c * 512:OFF_CONV + (c + 1) * 512])
    z_ref[...] = _dot(hb, w_ref[:, OFF_Z:OFF_B])
    gt_ref[...] = _dot3(h, wg_ref[...])


def _in_proj(x, norm1_g, w_main, w_vt, w_gates, tm):
    T = x.shape[0]
    assert tm % _VB == 0
    row = lambda n: pl.BlockSpec((tm, n), lambda i: (i, 0))
    full = lambda a: pl.BlockSpec(a.shape, lambda i: (0,) * a.ndim)
    out_shape = (
        jax.ShapeDtypeStruct((T, DA_QK), BF16),
        jax.ShapeDtypeStruct((T, DA_QK), BF16),
        jax.ShapeDtypeStruct((T, DA_QK), F32),
        jax.ShapeDtypeStruct((T, DA_QK), BF16),
        jax.ShapeDtypeStruct((T, DA_V), F32),
        jax.ShapeDtypeStruct((T, DA_V), BF16),
        jax.ShapeDtypeStruct((DA_HEADS, T // _VB, _VR, _VB), BF16),
        jax.ShapeDtypeStruct((T, GDN_CONV_CH), F32),
        jax.ShapeDtypeStruct((T, GDN_V), F32),
        jax.ShapeDtypeStruct((T, LANES), F32),
    )
    out_specs = tuple(
        pl.BlockSpec((DA_HEADS, tm // _VB, _VR, _VB), lambda i: (0, i, 0, 0)) if len(s.shape) == 4
        else row(s.shape[1]) for s in out_shape)
    return pl.pallas_call(
        _inproj_kernel,
        grid=(T // tm,),
        in_specs=[row(D_MODEL), full(norm1_g), full(w_main), full(w_vt), full(w_gates)],
        out_specs=out_specs,
        out_shape=out_shape,
        compiler_params=pltpu.CompilerParams(dimension_semantics=("arbitrary",), vmem_limit_bytes=VMEM_LIMIT),
        name="in_proj",
    )(x, norm1_g, w_main, w_vt, w_gates)


def _lam_from(lamv_ref, lam_init):
    lv = lamv_ref[...]
    s1 = jnp.sum(lv[0:1, :] * lv[1:2, :], axis=-1, keepdims=True)
    s2 = jnp.sum(lv[2:3, :] * lv[3:4, :], axis=-1, keepdims=True)
    return jnp.exp(s1) - jnp.exp(s2) + lam_init


def _stack_maps(q):
    lane = lax.broadcasted_iota(jnp.int32, q.shape, 1)
    zero = jnp.zeros_like(q)
    return jnp.concatenate([jnp.where(lane < DA_HEAD_DIM, q, zero), jnp.where(lane >= DA_HEAD_DIM, q, zero)], axis=0)


def _attn_finish(acc, l, lam, subg, lam_init, tq):
    o = acc[:tq] / l[:tq] - lam * (acc[tq:] / l[tq:])
    return _rms(o, subg) * (1.0 - lam_init)


_CT = 256


def _attn_prompt_kernel(lamv_ref, subg_ref, q_ref, k_ref, vt_ref, o_ref, qs_sc, m_sc, acc_sc, s_sc, pb_sc,
                        al_sc, *, tq, lam_init):
    i = pl.program_id(2)
    nq = 2 * tq
    nct = nq // _CT
    nvt = tq // _VB
    qt = jnp.transpose(q_ref[...].astype(F32))
    d = lax.broadcasted_iota(jnp.int32, qt.shape, 0)
    qs_sc[:, 0:tq] = jnp.where(d < DA_HEAD_DIM, qt, 0.0).astype(BF16)
    qs_sc[:, tq:nq] = jnp.where(d >= DA_HEAD_DIM, qt, 0.0).astype(BF16)
    m_sc[...] = jnp.full(m_sc.shape, NEG_INF, F32)
    acc_sc[...] = jnp.zeros(acc_sc.shape, F32)

    def kblock(j):
        return k_ref[pl.ds(pl.multiple_of(j * tq, tq), tq), :]

    def qk(kb, c):
        return _dot(kb, qs_sc[:, c * _CT:(c + 1) * _CT])

    def soft(c, s, masked):
        cs = slice(c * _CT, (c + 1) * _CT)
        if masked:
            key = lax.broadcasted_iota(jnp.int32, s.shape, 0)
            qry = lax.broadcasted_iota(jnp.int32, s.shape, 1) + ((c * _CT) % tq)
            s = jnp.where((qry // CHUNK) >= (key // CHUNK), s, NEG_INF)
        m_prev = m_sc[:, cs]
        m_new = jnp.maximum(m_prev, jnp.max(s, axis=0, keepdims=True))
        alpha = jnp.exp2(m_prev - m_new)
        p = jnp.exp2(s - m_new)
        m_sc[:, cs] = m_new
        return p.astype(BF16), alpha

    def pv(j, c, pb, alpha):
        cs = slice(c * _CT, (c + 1) * _CT)
        upd = _dot(vt_ref[j * nvt], pb[0:_VB, :])
        for t in range(1, nvt):
            upd = upd + _dot(vt_ref[j * nvt + t], pb[t * _VB:(t + 1) * _VB, :])
        acc_sc[:, cs] = alpha * acc_sc[:, cs] + upd

    s_sc[...] = qk(kblock(0), 0)
    pb_sc[...] = jnp.zeros(pb_sc.shape, BF16)
    al_sc[...] = jnp.ones(al_sc.shape, F32)

    def block(j, masked):
        kb = kblock(j)
        s_cur = s_sc[...]
        pend = None
        for c in range(nct):
            if c + 1 < nct:
                s_nxt = qk(kb, c + 1)
            elif not masked:
                s_sc[...] = qk(kblock(j + 1), 0)
            pb, alpha = soft(c, s_cur, masked)
            if c == 0:
                pv(jnp.maximum(j - 1, 0), nct - 1, pb_sc[...], al_sc[...])
            else:
                pv(j, c - 1, *pend)
            pend = (pb, alpha)
            if c + 1 < nct:
                s_cur = s_nxt
        pb_sc[...] = pend[0]
        al_sc[...] = pend[1]

    def body(j, carry):
        block(j, False)
        return carry

    lax.fori_loop(0, i, body, 0)
    block(i, True)
    pv(i, nct - 1, pb_sc[...], al_sc[...])
    lam = _lam_from(lamv_ref, lam_init)
    acc = acc_sc[0:DA_V_DIM, :]
    l = acc_sc[DA_V_DIM:DA_V_DIM + 1, :]
    ot = acc[:, 0:tq] / l[:, 0:tq] - lam * (acc[:, tq:nq] / l[:, tq:nq])
    ot = ot * lax.rsqrt(jnp.mean(ot * ot, axis=0, keepdims=True) + RMS_EPS) * subg_ref[...] * (1.0 - lam_init)
    o_ref[...] = jnp.transpose(ot).astype(o_ref.dtype)


def _attn_prompt(qb, kb, vt, lamv, subg_col, lam_init, tq):
    B, L, _ = qb.shape
    assert tq % _VB == 0 and (2 * tq) % _CT == 0 and L % tq == 0
    kern = functools.partial(_attn_prompt_kernel, tq=tq, lam_init=lam_init)
    nvb = L // _VB
    return pl.pallas_call(
        kern,
        grid=(B, DA_HEADS, L // tq),
        in_specs=[
            pl.BlockSpec(lamv.shape, lambda b, h, i: (0, 0)),
            pl.BlockSpec(subg_col.shape, lambda b, h, i: (0, 0)),
            pl.BlockSpec((None, tq, LANES), lambda b, h, i: (b, i, h)),
            pl.BlockSpec((None, L, LANES), lambda b, h, i: (b, 0, h)),
            pl.BlockSpec((None, nvb, _VR, _VB), lambda b, h, i: (h, b, 0, 0)),
        ],
        out_specs=pl.BlockSpec((None, tq, LANES), lambda b, h, i: (b, i, h)),
        out_shape=jax.ShapeDtypeStruct((B, L, DA_V), BF16),
        scratch_shapes=[
            pltpu.VMEM((LANES, 2 * tq), BF16),
            pltpu.VMEM((1, 2 * tq), F32),
            pltpu.VMEM((_VR, 2 * tq), F32),
            pltpu.VMEM((tq, _CT), F32),
            pltpu.VMEM((tq, _CT), BF16),
            pltpu.VMEM((1, _CT), F32),
        ],
        compiler_params=pltpu.CompilerParams(
            dimension_semantics=("arbitrary", "arbitrary", "arbitrary"), vmem_limit_bytes=VMEM_LIMIT),
        name="attn_prompt",
    )(lamv, subg_col, qb, kb, vt)


def _attn_sample_kernel(lamv_ref, subg_ref, q_ref, kp_ref, vp_ref, kn_ref, vn_ref, o_ref, *, lam_init):
    tq = q_ref.shape[0]
    qs = _stack_maps(q_ref[...])
    sp = _dot(qs, kp_ref[...].astype(BF16), _NT)
    sn = _dot(qs, kn_ref[...], _NT)
    m = jnp.maximum(jnp.max(sp, axis=1, keepdims=True), jnp.max(sn, axis=1, keepdims=True))
    pp = jnp.exp(sp - m)
    pn = jnp.exp(sn - m)
    l = jnp.sum(pp, axis=1, keepdims=True) + jnp.sum(pn, axis=1, keepdims=True)
    acc = _dot(pp.astype(BF16), vp_ref[...].astype(BF16)) + _dot(pn.astype(BF16), vn_ref[...])
    lam = _lam_from(lamv_ref, lam_init)
    o_ref[...] = _attn_finish(acc, l, lam, subg_ref[...], lam_init, tq).astype(o_ref.dtype)


def _attn_sample(qb, kb, vb, cache_k, cache_v, lamv, subg, lam_init):
    B, L, _ = qb.shape
    P = cache_k.shape[1]
    kern = functools.partial(_attn_sample_kernel, lam_init=lam_init)
    new = pl.BlockSpec((None, L, LANES), lambda b, h: (b, 0, h))
    past = pl.BlockSpec((None, P, LANES), lambda b, h: (b, 0, h))
    return pl.pallas_call(
        kern,
        grid=(B, DA_HEADS),
        in_specs=[
            pl.BlockSpec(lamv.shape, lambda b, h: (0, 0)),
            pl.BlockSpec(subg.shape, lambda b, h: (0, 0)),
            new, past, past, new, new,
        ],
        out_specs=new,
        out_shape=jax.ShapeDtypeStruct((B, L, DA_V), BF16),
        compiler_params=pltpu.CompilerParams(dimension_semantics=("arbitrary", "arbitrary")),
        name="attn_sample",
    )(lamv, subg, qb, cache_k, cache_v, kb, vb)


_PAD = 8


def _gdn_kernel(u_ref, gt_ref, z_ref, c0_ref, s0_ref, cw_ref, alog_ref, dtb_ref, gn_ref,
                og_ref, sout_ref, ubuf, s_sc, *, nchunks):
    c = pl.program_id(1)
    nc = pl.num_programs(1)
    C = CHUNK
    R = C * nchunks

    @pl.when(c == 0)
    def _():
        ubuf[0:_PAD, :] = c0_ref[...]
        s_sc[...] = s0_ref[...]

    u = u_ref[...]
    ubuf[_PAD:_PAD + R, :] = u
    cw = cw_ref[...]
    y = u * cw[3:4, :]
    for j in range(1, GDN_CONV):
        y = y + ubuf[_PAD - j:_PAD - j + R, :] * cw[3 - j:4 - j, :]
    ubuf[0:_PAD, :] = u[R - _PAD:, :]
    cq = y * jax.nn.sigmoid(y)

    gt = gt_ref[...]
    beta_all = jax.nn.sigmoid(gt)
    xg = gt + dtb_ref[...]
    softplus = jnp.maximum(xg, 0.0) + jnp.log1p(jnp.exp(-jnp.abs(xg)))
    g_all = -jnp.exp(alog_ref[...]) * softplus
    row = lax.broadcasted_iota(jnp.int32, (C, C), 0)
    col = lax.broadcasted_iota(jnp.int32, (C, C), 1)
    causal = row >= col
    strict = row > col
    eye = jnp.where(row == col, 1.0, 0.0).astype(F32)
    rr = lax.broadcasted_iota(jnp.int32, (R, R), 0)
    cc = lax.broadcasted_iota(jnp.int32, (R, R), 1)
    trilb = jnp.where(jnp.logical_and(rr >= cc, (rr // C) == (cc // C)), 1.0, 0.0).astype(BF16)
    ghi = g_all.astype(BF16)
    r1 = g_all - ghi.astype(F32)
    gmid = r1.astype(BF16)
    glo = (r1 - gmid.astype(F32)).astype(BF16)
    G_all = _dot(trilb, ghi) + (_dot(trilb, gmid) + _dot(trilb, glo))
    G_t = jnp.transpose(G_all)
    gn = gn_ref[...]

    qn, kn = [], []
    for h in range(GDN_HEADS):
        q = cq[:, h * GDN_HEAD_DIM:(h + 1) * GDN_HEAD_DIM]
        k = cq[:, GDN_K + h * GDN_HEAD_DIM:GDN_K + (h + 1) * GDN_HEAD_DIM]
        qn.append(q * lax.rsqrt(jnp.sum(q * q, axis=-1, keepdims=True) + 1e-6) * (GDN_HEAD_DIM ** -0.5))
        kn.append(k * lax.rsqrt(jnp.sum(k * k, axis=-1, keepdims=True) + 1e-6))

    keys = [(n, h) for n in range(nchunks) for h in range(GDN_HEADS)]
    st = {}
    for (n, h) in keys:
        rs = slice(n * C, (n + 1) * C)
        q = qn[h][rs]
        k = kn[h][rs]
        v = cq[rs, 2 * GDN_K + h * GDN_HEAD_DIM:2 * GDN_K + (h + 1) * GDN_HEAD_DIM]
        beta = beta_all[rs, h:h + 1]
        Gc = G_all[rs, GDN_HEADS + h:GDN_HEADS + h + 1]
        Gr = G_t[GDN_HEADS + h:GDN_HEADS + h + 1, rs]
        Gl = Gc[C - 1:C, :]
        decay = jnp.where(causal, jnp.exp(jnp.where(causal, Gc - Gr, 0.0)), 0.0)
        eG = jnp.exp(Gc)
        kb = k * beta
        st[n, h] = dict(decay=decay, kbf=k.astype(BF16), kkb=kb.astype(BF16), qb=q.astype(BF16),
                        rhs=jnp.concatenate([v * beta, kb * eG], axis=1),
                        qg=(q * eG).astype(BF16), kg=k * jnp.exp(Gl - Gc), gl=jnp.exp(Gl))
    for key in keys:
        d = st[key]
        d["kk"] = _dot(d["kkb"], d["kbf"], _NT)
    for key in keys:
        d = st[key]
        d["qk"] = (_dot(d["qb"], d["kbf"], _NT) * d["decay"]).astype(BF16)
    for key in keys:
        d = st[key]
        nm = jnp.where(strict, d["kk"] * d["decay"], 0.0)
        d["inv"] = eye - nm
        d["pw"] = nm.astype(BF16)
    for _ in range(5):
        for key in keys:
            d = st[key]
            d["pwf"] = _dot(d["pw"], d["pw"])
        for key in keys:
            d = st[key]
            d["pw"] = d["pwf"].astype(BF16)
            d["inv"] = d["inv"] + _dot(d["inv"].astype(BF16), d["pw"])
    for key in keys:
        d = st[key]
        invb = d["inv"].astype(BF16)
        rh, rl = _split_bf16(d["rhs"])
        sol = _dot(invb, rh) + _dot(invb, rl)
        d["u"] = sol[:, :GDN_HEAD_DIM]
        d["w"] = sol[:, GDN_HEAD_DIM:].astype(BF16)
        d["kgt"] = jnp.transpose(d["kg"]).astype(BF16)

    S = [s_sc[h] for h in range(GDN_HEADS)]
    heads = range(GDN_HEADS)
    for n in range(nchunks):
        rs = slice(n * C, (n + 1) * C)
        Sb = [S[h].astype(BF16) for h in heads]
        ws = [_dot(st[n, h]["w"], Sb[h]) for h in heads]
        qs = [_dot(st[n, h]["qg"], Sb[h]) for h in heads]
        vb = [(st[n, h]["u"] - ws[h]).astype(BF16) for h in heads]
        o = [qs[h] + _dot(st[n, h]["qk"], vb[h]) for h in heads]
        for h in heads:
            S[h] = S[h] * st[n, h]["gl"] + _dot(st[n, h]["kgt"], vb[h])
        for h in heads:
            sl = slice(h * GDN_HEAD_DIM, (h + 1) * GDN_HEAD_DIM)
            zz = z_ref[rs, sl]
            og_ref[rs, sl] = (_rms(o[h], gn) * (zz * jax.nn.sigmoid(zz))).astype(og_ref.dtype)
    for h in heads:
        s_sc[h] = S[h]

    @pl.when(c == nc - 1)
    def _():
        sout_ref[...] = s_sc[...]


def _gdn(conv_in, gates, z, conv0p, s0, conv_w, alog_p, dtb_p, gn, nchunks):
    B, L, _ = conv_in.shape
    R = CHUNK * nchunks
    assert L % R == 0
    tok = lambda n: pl.BlockSpec((None, R, n), lambda b, c: (b, c, 0))
    full = lambda a: pl.BlockSpec(a.shape, lambda b, c: (0,) * a.ndim)
    return pl.pallas_call(
        functools.partial(_gdn_kernel, nchunks=nchunks),
        grid=(B, L // R),
        in_specs=[
            tok(GDN_CONV_CH), tok(LANES), tok(GDN_V),
            pl.BlockSpec((None, _PAD, GDN_CONV_CH), lambda b, c: (b, 0, 0)),
            pl.BlockSpec((None, GDN_HEADS, GDN_HEAD_DIM, GDN_HEAD_DIM), lambda b, c: (b, 0, 0, 0)),
            full(conv_w), full(alog_p), full(dtb_p), full(gn),
        ],
        out_specs=(
            tok(GDN_V),
            pl.BlockSpec((None, GDN_HEADS, GDN_HEAD_DIM, GDN_HEAD_DIM), lambda b, c: (b, 0, 0, 0)),
        ),
        out_shape=(
            jax.ShapeDtypeStruct((B, L, GDN_V), BF16),
            jax.ShapeDtypeStruct((B, GDN_HEADS, GDN_HEAD_DIM, GDN_HEAD_DIM), F32),
        ),
        scratch_shapes=[
            pltpu.VMEM((_PAD + R, GDN_CONV_CH), F32),
            pltpu.VMEM((GDN_HEADS, GDN_HEAD_DIM, GDN_HEAD_DIM), F32),
        ],
        compiler_params=pltpu.CompilerParams(dimension_semantics=("arbitrary", "arbitrary"),
                                             vmem_limit_bytes=VMEM_LIMIT),
        name="gdn",
    )(conv_in, gates, z, conv0p, s0, conv_w, alog_p, dtb_p, gn)


def _postmix_kernel(x_ref, oa_ref, og_ref, wo_ref, g2_ref, wr_ref, br_ref, x1_ref, h2_ref, comb_ref):
    mix = _dot(oa_ref[...], wo_ref[0:DA_V, :]) + _dot(og_ref[...], wo_ref[DA_V:, :])
    x1 = x_ref[...] + mix
    x1_ref[...] = x1
    h2 = _rms(x1, g2_ref[...])
    h2_ref[...] = h2.astype(BF16)
    logits = _dot3(h2, wr_ref[...]) + br_ref[...]
    lane = lax.broadcasted_iota(jnp.int32, logits.shape, 1).astype(F32)
    work = logits
    sel = jnp.zeros(logits.shape, jnp.bool_)
    top = jnp.max(work, axis=1, keepdims=True)
    for _ in range(TOP_K):
        mx = jnp.max(work, axis=1, keepdims=True)
        first = jnp.min(jnp.where(work == mx, lane, float(LANES)), axis=1, keepdims=True)
        pick = lane == first
        sel = jnp.logical_or(sel, pick)
        work = jnp.where(pick, -jnp.inf, work)
    e = jnp.where(sel, jnp.exp(logits - top), 0.0)
    comb_ref[...] = e / jnp.sum(e, axis=1, keepdims=True)


def _post_mix(x, oa, og, wo, g2, wr, br, tm):
    T = x.shape[0]
    row = lambda n: pl.BlockSpec((tm, n), lambda i: (i, 0))
    full = lambda a: pl.BlockSpec(a.shape, lambda i: (0,) * a.ndim)
    return pl.pallas_call(
        _postmix_kernel,
        grid=(T // tm,),
        in_specs=[row(D_MODEL), row(DA_V), row(GDN_V), full(wo), full(g2), full(wr), full(br)],
        out_specs=(row(D_MODEL), row(D_MODEL), row(LANES)),
        out_shape=(
            jax.ShapeDtypeStruct((T, D_MODEL), F32),
            jax.ShapeDtypeStruct((T, D_MODEL), BF16),
            jax.ShapeDtypeStruct((T, LANES), F32),
        ),
        compiler_params=pltpu.CompilerParams(dimension_semantics=("arbitrary",), vmem_limit_bytes=VMEM_LIMIT),
        name="post_mix",
    )(x, oa, og, wo, g2, wr, br)


_FF_BLK = 512


def _moe_kernel(h_ref, comb_ref, x1_ref, wg_ref, bg_ref, wu_ref, bu_ref, wd_ref, bd_ref, fg_ref, y_ref, acc_sc):
    e = pl.program_id(1)

    @pl.when(e == 0)
    def _():
        acc_sc[...] = jnp.zeros(acc_sc.shape, F32)

    h = h_ref[...]
    o = bd_ref[...]
    for f in range(D_FF // _FF_BLK):
        fs = slice(f * _FF_BLK, (f + 1) * _FF_BLK)
        gt = jnp.minimum(_dot(h, wg_ref[:, fs]) + bg_ref[:, fs], SWIGLU_LIMIT)
        up = jnp.clip(_dot(h, wu_ref[:, fs]) + bu_ref[:, fs], -SWIGLU_LIMIT, SWIGLU_LIMIT)
        act = (up + 1.0) * gt * jax.nn.sigmoid(SWIGLU_ALPHA * gt)
        o = o + _dot(act.astype(BF16), wd_ref[fs, :])
    comb = comb_ref[...]
    lane = lax.broadcasted_iota(jnp.int32, comb.shape, 1)
    ce = jnp.sum(jnp.where(lane == e, comb, 0.0), axis=1, keepdims=True)
    acc_sc[...] = acc_sc[...] + ce * o

    @pl.when(e == pl.num_programs(1) - 1)
    def _():
        y_ref[...] = _rms(x1_ref[...] + acc_sc[...], fg_ref[...])


def _moe(h2, comb, x1, wg, bg, wu, bu, wd, bd, fg, tm):
    T = h2.shape[0]
    row = lambda n: pl.BlockSpec((tm, n), lambda i, e: (i, 0))
    wspec = pl.BlockSpec((None, D_MODEL, D_FF), lambda i, e: (e, 0, 0))
    bspec = pl.BlockSpec((None, 1, D_FF), lambda i, e: (e, 0, 0))
    return pl.pallas_call(
        _moe_kernel,
        grid=(T // tm, N_EXPERTS),
        in_specs=[row(D_MODEL), row(LANES), row(D_MODEL), wspec, bspec, wspec, bspec, wspec, bspec,
                  pl.BlockSpec(fg.shape, lambda i, e: (0, 0))],
        out_specs=row(D_MODEL),
        out_shape=jax.ShapeDtypeStruct((T, D_MODEL), F32),
        scratch_shapes=[pltpu.VMEM((tm, D_MODEL), F32)],
        compiler_params=pltpu.CompilerParams(
            dimension_semantics=("arbitrary", "arbitrary"), vmem_limit_bytes=VMEM_LIMIT),
        name="moe",
    )(h2, comb, x1, wg, bg, wu, bu, wd, bd, fg)


def _pick(n, prefs):
    for t in prefs:
        if n % t == 0:
            return t
    raise ValueError(f"no tile for {n}")


def _layer(x, k_past, v_past, S0, conv0, lam_init, wts):
    (norm1_g, w_main, w_vt, w_gates, lamv, subg, conv_w, alog_p, dtb_p, gn, wo, g2, wr, br,
     wg, bg, wu, bu, wd, bd) = wts
    B, L, _ = x.shape
    T = B * L
    xt = x.reshape(T, D_MODEL)
    tm = _pick(T, (256,))
    qb, q2, k, kb, v, vb, vt, conv_in, z, gates = _in_proj(xt, norm1_g, w_main, w_vt, w_gates, tm)
    r3 = lambda a: a.reshape(B, L, a.shape[-1])
    if k_past is None:
        oa = _attn_prompt(r3(q2), r3(kb), vt, lamv, subg.reshape(DA_V_DIM, 1), lam_init, _pick(L, (512, 256)))
    else:
        P = k_past.shape[1]
        oa = _attn_sample(r3(qb), r3(kb), r3(vb), k_past.reshape(B, P, DA_QK), v_past.reshape(B, P, DA_V),
                          lamv, subg, lam_init)
    conv_in3 = r3(conv_in)
    conv0p = jnp.concatenate([jnp.zeros((B, _PAD - (GDN_CONV - 1), GDN_CONV_CH), F32), conv0.astype(F32)], axis=1)
    og, s_new = _gdn(conv_in3, r3(gates), r3(z), conv0p, S0.astype(F32), conv_w, alog_p, dtb_p, gn,
                     _pick(L // CHUNK, (4, 2, 1)))
    x1, h2, comb = _post_mix(xt, oa.reshape(T, DA_V), og.reshape(T, GDN_V), wo, g2, wr, br, tm)
    return (x1, h2, comb, k.reshape(B, L, DA_HEADS, 2, DA_HEAD_DIM), v.reshape(B, L, DA_HEADS, DA_V_DIM),
            s_new, conv_in3[:, L - (GDN_CONV - 1):, :])


def kernel(x_prompt, x_sample, cache_k, cache_v, state_ssm, state_conv, norm1_g, w_in, lambda_q1, lambda_k1,
           lambda_q2, lambda_k2, da_subln_g, conv_w, a_log, dt_bias, gdn_norm_g, w_out, norm2_g, w_router,
           b_router, w_gate, b_gate, w_up, b_up, w_down, b_down, final_g):
    depth = w_in.shape[0]
    assert depth == 1, "single-layer trunk"
    l = 0
    lam_init = 0.8 - 0.6 * math.exp(-0.3 * l)
    bp = x_prompt.shape[0]
    pad_lanes = lambda a, fill=0.0: jnp.concatenate(
        [a.astype(F32), jnp.full(a.shape[:-1] + (LANES - a.shape[-1],), fill, F32)], axis=-1)
    w_gates = pad_lanes(w_in[l][:, OFF_B:])
    zeros4 = jnp.zeros((GDN_HEADS,), F32)
    alog_p = pad_lanes(jnp.concatenate([zeros4, a_log[l].astype(F32)])[None, :])
    dtb_p = pad_lanes(jnp.concatenate([zeros4, dt_bias[l].astype(F32)])[None, :])
    wts = (
        norm1_g[l][None, :].astype(F32),
        w_in[l][:, :OFF_B].astype(BF16),
        w_in[l][:, OFF_VA:OFF_CONV].T.astype(BF16),
        w_gates,
        jnp.stack([lambda_q1[l], lambda_k1[l], lambda_q2[l], lambda_k2[l]]).astype(F32),
        da_subln_g[l][None, :].astype(F32),
        conv_w[l].astype(F32),
        alog_p, dtb_p,
        gdn_norm_g[l][None, :].astype(F32),
        w_out[l].astype(BF16),
        norm2_g[l][None, :].astype(F32),
        pad_lanes(w_router[l]),
        pad_lanes(b_router[l][None, :], NEG_INF),
        w_gate[l].astype(BF16), b_gate[l][:, None, :].astype(F32),
        w_up[l].astype(BF16), b_up[l][:, None, :].astype(F32),
        w_down[l].astype(BF16), b_down[l][:, None, :].astype(F32),
    )
    moe_w = wts[14:]
    fg = final_g[None, :].astype(F32)

    s0 = jnp.zeros((bp, GDN_HEADS, GDN_HEAD_DIM, GDN_HEAD_DIM), x_prompt.dtype)
    c0 = jnp.zeros((bp, GDN_CONV - 1, GDN_CONV_CH), x_prompt.dtype)
    x1p, h2p, combp, k1, v1, s1, c1 = _layer(x_prompt, None, None, s0, c0, lam_init, wts)
    x1s, h2s, combs, k2, v2, s2, c2 = _layer(x_sample, cache_k[l], cache_v[l], state_ssm[l], state_conv[l],
                                             lam_init, wts)
    y_p = _moe(h2p, combp, x1p, *moe_w, fg, _pick(h2p.shape[0], (1024, 512, 256, 128, 64)))
    y_s = _moe(h2s, combs, x1s, *moe_w, fg, _pick(h2s.shape[0], (1024, 512, 256, 128, 64)))
    st = lambda a: a[None]
    return (y_p.reshape(x_prompt.shape), y_s.reshape(x_sample.shape), st(k1), st(v1), st(s1), st(c1),
            st(k2), st(v2), st(s2), st(c2))
```

```python
import functools
import math

import jax
import jax.numpy as jnp
from jax import lax
from jax.experimental import pallas as pl
from jax.experimental.pallas import tpu as pltpu
from jax.experimental.pallas import tpu_sc as plsc

F32 = jnp.float32
BF16 = jnp.bfloat16

D_MODEL = 1024
CHUNK = 64
RMS_EPS = 1e-6
NEG_INF = -1e30
DA_HEADS = 4
DA_HEAD_DIM = 64
DA_V_DIM = 128
DA_QK = 512
DA_V = 512
GDN_HEADS = 4
GDN_HEAD_DIM = 128
GDN_K = 512
GDN_V = 512
GDN_CONV = 4
GDN_CONV_CH = 1536
OFF_KA = DA_QK
OFF_VA = 2 * DA_QK
OFF_CONV = 2 * DA_QK + DA_V
OFF_Z = OFF_CONV + GDN_CONV_CH
OFF_B = OFF_Z + GDN_V
N_EXPERTS = 32
TOP_K = 4
D_FF = 1024
SWIGLU_LIMIT = 7.0
SWIGLU_ALPHA = 1.702
LANES = 128
VMEM_LIMIT = 56 * 1024 * 1024
LOG2E = 1.4426950408889634


def _split_bf16(a):
    hi = a.astype(BF16)
    lo = (a - hi.astype(F32)).astype(BF16)
    return hi, lo


def _dot(a, b, dims=(((1,), (0,)), ((), ()))):
    return lax.dot_general(a, b, dims, preferred_element_type=F32)


def _dot3(a, b, dims=(((1,), (0,)), ((), ()))):
    ah, al = _split_bf16(a)
    bh, bl = _split_bf16(b)
    return _dot(ah, bh, dims) + (_dot(ah, bl, dims) + _dot(al, bh, dims))


_NT = (((1,), (1,)), ((), ()))


def _rms(x, g):
    return x * lax.rsqrt(jnp.mean(x * x, axis=-1, keepdims=True) + RMS_EPS) * g


def _pack_bf16_pairs(x):
    n = x.shape[1] // 2
    lo = lax.bitcast_convert_type(x[:, :n].astype(BF16).astype(F32), jnp.uint32)
    hi = lax.bitcast_convert_type(x[:, n:].astype(BF16).astype(F32), jnp.uint32)
    return jnp.bitwise_or(lax.shift_right_logical(lo, jnp.uint32(16)), hi)


def _unpack_bf16_pairs(w):
    lo = lax.bitcast_convert_type(lax.shift_left(w, jnp.uint32(16)), F32)
    hi = lax.bitcast_convert_type(jnp.bitwise_and(w, jnp.uint32(0xFFFF0000)), F32)
    return jnp.concatenate([lo, hi], axis=1)


_VB = 256
_VR = DA_V_DIM + 16


def _inproj_kernel(x_ref, g_ref, w_ref, wvt_ref, wg_ref,
                   qb_ref, q2_ref, k_ref, kb_ref, v_ref, vb_ref, vt_ref, cv_ref, z_ref, gt_ref):
    h = _rms(x_ref[...], g_ref[...])
    hb = h.astype(BF16)
    q = _dot(hb, w_ref[:, 0:OFF_KA])
    qb_ref[...] = (q * (DA_HEAD_DIM ** -0.5)).astype(BF16)
    q2_ref[...] = (q * (DA_HEAD_DIM ** -0.5 * LOG2E)).astype(BF16)
    k = _dot(hb, w_ref[:, OFF_KA:OFF_VA])
    k_ref[...] = k
    kb_ref[...] = k.astype(BF16)
    v = _dot(hb, w_ref[:, OFF_VA:OFF_CONV])
    v_ref[...] = v
    vb_ref[...] = v.astype(BF16)
    vt = _dot(wvt_ref[...], hb, _NT).astype(BF16)
    sub = lax.broadcasted_iota(jnp.int32, (_VR - DA_V_DIM, _VB), 0)
    ones_row = jnp.where(sub == 0, 1.0, 0.0).astype(BF16)
    for hd in range(DA_HEADS):
        for t in range(vt.shape[1] // _VB):
            vt_ref[hd, t, 0:DA_V_DIM, :] = vt[hd * DA_V_DIM:(hd + 1) * DA_V_DIM, t * _VB:(t + 1) * _VB]
            vt_ref[hd, t, DA_V_DIM:_VR, :] = ones_row
    for c in range(GDN_CONV_CH // 512):
        cv_ref[:, c * 512:(c + 1) * 512] = _dot(hb, w_ref[:, OFF_CONV + c * 512:OFF_CONV + (c + 1) * 512])
    z_ref[...] = _dot(hb, w_ref[:, OFF_Z:OFF_B])
    gt_ref[...] = _dot3(h, wg_ref[...])


def _in_proj(x, norm1_g, w_main, w_vt, w_gates, tm):
    T = x.shape[0]
    assert tm % _VB == 0
    row = lambda n: pl.BlockSpec((tm, n), lambda i: (i, 0))
    full = lambda a: pl.BlockSpec(a.shape, lambda i: (0,) * a.ndim)
    out_shape = (
        jax.ShapeDtypeStruct((T, DA_QK), BF16),
        jax.ShapeDtypeStruct((T, DA_QK), BF16),
        jax.ShapeDtypeStruct((T, DA_QK), F32),
        jax.ShapeDtypeStruct((T, DA_QK), BF16),
        jax.ShapeDtypeStruct((T, DA_V), F32),
        jax.ShapeDtypeStruct((T, DA_V), BF16),
        jax.ShapeDtypeStruct((DA_HEADS, T // _VB, _VR, _VB), BF16),
        jax.ShapeDtypeStruct((T, GDN_CONV_CH), F32),
        jax.ShapeDtypeStruct((T, GDN_V), F32),
        jax.ShapeDtypeStruct((T, LANES), F32),
    )
    out_specs = tuple(
        pl.BlockSpec((DA_HEADS, tm // _VB, _VR, _VB), lambda i: (0, i, 0, 0)) if len(s.shape) == 4
        else row(s.shape[1]) for s in out_shape)
    return pl.pallas_call(
        _inproj_kernel,
        grid=(T // tm,),
        in_specs=[row(D_MODEL), full(norm1_g), full(w_main), full(w_vt), full(w_gates)],
        out_specs=out_specs,
        out_shape=out_shape,
        compiler_params=pltpu.CompilerParams(dimension_semantics=("arbitrary",), vmem_limit_bytes=VMEM_LIMIT),
        name="in_proj",
    )(x, norm1_g, w_main, w_vt, w_gates)


def _lam_from(lamv_ref, lam_init):
    lv = lamv_ref[...]
    s1 = jnp.sum(lv[0:1, :] * lv[1:2, :], axis=-1, keepdims=True)
    s2 = jnp.sum(lv[2:3, :] * lv[3:4, :], axis=-1, keepdims=True)
    return jnp.exp(s1) - jnp.exp(s2) + lam_init


def _stack_maps(q):
    lane = lax.broadcasted_iota(jnp.int32, q.shape, 1)
    zero = jnp.zeros_like(q)
    return jnp.concatenate([jnp.where(lane < DA_HEAD_DIM, q, zero), jnp.where(lane >= DA_HEAD_DIM, q, zero)], axis=0)


def _attn_finish(acc, l, lam, subg, lam_init, tq):
    o = acc[:tq] / l[:tq] - lam * (acc[tq:] / l[tq:])
    return _rms(o, subg) * (1.0 - lam_init)


_CT = 256


def _attn_prompt_kernel(lamv_ref, subg_ref, q_ref, k_ref, vt_ref, o_ref, qs_sc, m_sc, acc_sc, s_sc, pb_sc,
                        al_sc, *, tq, lam_init):
    i = pl.program_id(2)
    nq = 2 * tq
    nct = nq // _CT
    nvt = tq // _VB
    qt = jnp.transpose(q_ref[...].astype(F32))
    d = lax.broadcasted_iota(jnp.int32, qt.shape, 0)
    qs_sc[:, 0:tq] = jnp.where(d < DA_HEAD_DIM, qt, 0.0).astype(BF16)
    qs_sc[:, tq:nq] = jnp.where(d >= DA_HEAD_DIM, qt, 0.0).astype(BF16)
    m_sc[...] = jnp.full(m_sc.shape, NEG_INF, F32)
    acc_sc[...] = jnp.zeros(acc_sc.shape, F32)

    def kblock(j):
        return k_ref[pl.ds(pl.multiple_of(j * tq, tq), tq), :]

    def qk(kb, c):
        return _dot(kb, qs_sc[:, c * _CT:(c + 1) * _CT])

    def soft(c, s, masked):
        cs = slice(c * _CT, (c + 1) * _CT)
        if masked:
            key = lax.broadcasted_iota(jnp.int32, s.shape, 0)
            qry = lax.broadcasted_iota(jnp.int32, s.shape, 1) + ((c * _CT) % tq)
            s = jnp.where((qry // CHUNK) >= (key // CHUNK), s, NEG_INF)
        m_prev = m_sc[:, cs]
        m_new = jnp.maximum(m_prev, jnp.max(s, axis=0, keepdims=True))
        alpha = jnp.exp2(m_prev - m_new)
        p = jnp.exp2(s - m_new)
        m_sc[:, cs] = m_new
        return p.astype(BF16), alpha

    def pv(j, c, pb, alpha):
        cs = slice(c * _CT, (c + 1) * _CT)
        upd = _dot(vt_ref[j * nvt], pb[0:_VB, :])
        for t in range(1, nvt):
            upd = upd + _dot(vt_ref[j * nvt + t], pb[t * _VB:(t + 1) * _VB, :])
        acc_sc[:, cs] = alpha * acc_sc[:, cs] + upd

    s_sc[...] = qk(kblock(0), 0)
    pb_sc[...] = jnp.zeros(pb_sc.shape, BF16)
    al_sc[...] = jnp.ones(al_sc.shape, F32)

    def block(j, masked):
        kb = kblock(j)
        s_cur = s_sc[...]
        pend = None
        for c in range(nct):
            if c + 1 < nct:
                s_nxt = qk(kb, c + 1)
            elif not masked:
                s_sc[...] = qk(kblock(j + 1), 0)
            pb, alpha = soft(c, s_cur, masked)
            if c == 0:
                pv(jnp.maximum(j - 1, 0), nct - 1, pb_sc[...], al_sc[...])
            else:
                pv(j, c - 1, *pend)
            pend = (pb, alpha)
            if c + 1 < nct:
                s_cur = s_nxt
        pb_sc[...] = pend[0]
        al_sc[...] = pend[1]

    def body(j, carry):
        block(j, False)
        return carry

    lax.fori_loop(0, i, body, 0)
    block(i, True)
    pv(i, nct - 1, pb_sc[...], al_sc[...])
    lam = _lam_from(lamv_ref, lam_init)
    acc = acc_sc[0:DA_V_DIM, :]
    l = acc_sc[DA_V_DIM:DA_V_DIM + 1, :]
    ot = acc[:, 0:tq] / l[:, 0:tq] - lam * (acc[:, tq:nq] / l[:, tq:nq])
    ot = ot * lax.rsqrt(jnp.mean(ot * ot, axis=0, keepdims=True) + RMS_EPS) * subg_ref[...] * (1.0 - lam_init)
    o_ref[...] = jnp.transpose(ot).astype(o_ref.dtype)


def _attn_prompt(qb, kb, vt, lamv, subg_col, lam_init, tq):
    B, L, _ = qb.shape
    assert tq % _VB == 0 and (2 * tq) % _CT == 0 and L % tq == 0
    kern = functools.partial(_attn_prompt_kernel, tq=tq, lam_init=lam_init)
    nvb = L // _VB
    return pl.pallas_call(
        kern,
        grid=(B, DA_HEADS, L // tq),
        in_specs=[
            pl.BlockSpec(lamv.shape, lambda b, h, i: (0, 0)),
            pl.BlockSpec(subg_col.shape, lambda b, h, i: (0, 0)),
            pl.BlockSpec((None, tq, LANES), lambda b, h, i: (b, i, h)),
            pl.BlockSpec((None, L, LANES), lambda b, h, i: (b, 0, h)),
            pl.BlockSpec((None, nvb, _VR, _VB), lambda b, h, i: (h, b, 0, 0)),
        ],
        out_specs=pl.BlockSpec((None, tq, LANES), lambda b, h, i: (b, i, h)),
        out_shape=jax.ShapeDtypeStruct((B, L, DA_V), BF16),
        scratch_shapes=[
            pltpu.VMEM((LANES, 2 * tq), BF16),
            pltpu.VMEM((1, 2 * tq), F32),
            pltpu.VMEM((_VR, 2 * tq), F32),
            pltpu.VMEM((tq, _CT), F32),
            pltpu.VMEM((tq, _CT), BF16),
            pltpu.VMEM((1, _CT), F32),
        ],
        compiler_params=pltpu.CompilerParams(
            dimension_semantics=("arbitrary", "arbitrary", "arbitrary"), vmem_limit_bytes=VMEM_LIMIT),
        name="attn_prompt",
    )(lamv, subg_col, qb, kb, vt)


def _attn_sample_kernel(lamv_ref, subg_ref, q_ref, kp_ref, vp_ref, kn_ref, vn_ref, o_ref, *, lam_init):
    tq = q_ref.shape[0]
    qs = _stack_maps(q_ref[...])
    sp = _dot(qs, kp_ref[...].astype(BF16), _NT)
    sn = _dot(qs, kn_ref[...], _NT)
    m = jnp.maximum(jnp.max(sp, axis=1, keepdims=True), jnp.max(sn, axis=1, keepdims=True))
    pp = jnp.exp(sp - m)
    pn = jnp.exp(sn - m)
    l = jnp.sum(pp, axis=1, keepdims=True) + jnp.sum(pn, axis=1, keepdims=True)
    acc = _dot(pp.astype(BF16), vp_ref[...].astype(BF16)) + _dot(pn.astype(BF16), vn_ref[...])
    lam = _lam_from(lamv_ref, lam_init)
    o_ref[...] = _attn_finish(acc, l, lam, subg_ref[...], lam_init, tq).astype(o_ref.dtype)


def _attn_sample(qb, kb, vb, cache_k, cache_v, lamv, subg, lam_init):
    B, L, _ = qb.shape
    P = cache_k.shape[1]
    kern = functools.partial(_attn_sample_kernel, lam_init=lam_init)
    new = pl.BlockSpec((None, L, LANES), lambda b, h: (b, 0, h))
    past = pl.BlockSpec((None, P, LANES), lambda b, h: (b, 0, h))
    return pl.pallas_call(
        kern,
        grid=(B, DA_HEADS),
        in_specs=[
            pl.BlockSpec(lamv.shape, lambda b, h: (0, 0)),
            pl.BlockSpec(subg.shape, lambda b, h: (0, 0)),
            new, past, past, new, new,
        ],
        out_specs=new,
        out_shape=jax.ShapeDtypeStruct((B, L, DA_V), BF16),
        compiler_params=pltpu.CompilerParams(dimension_semantics=("arbitrary", "arbitrary")),
        name="attn_sample",
    )(lamv, subg, qb, cache_k, cache_v, kb, vb)


_PAD = 8


def _gdn_kernel(u_ref, gt_ref, z_ref, c0_ref, s0_ref, cw_ref, alog_ref, dtb_ref, gn_ref,
                og_ref, sout_ref, ubuf, s_sc, *, nchunks):
    c = pl.program_id(1)
    nc = pl.num_programs(1)
    C = CHUNK
    R = C * nchunks

    @pl.when(c == 0)
    def _():
        ubuf[0:_PAD, :] = c0_ref[...]
        s_sc[...] = s0_ref[...]

    u = u_ref[...]
    ubuf[_PAD:_PAD + R, :] = u
    cw = cw_ref[...]
    y = u * cw[3:4, :]
    for j in range(1, GDN_CONV):
        y = y + ubuf[_PAD - j:_PAD - j + R, :] * cw[3 - j:4 - j, :]
    ubuf[0:_PAD, :] = u[R - _PAD:, :]
    cq = y * jax.nn.sigmoid(y)

    gt = gt_ref[...]
    beta_all = jax.nn.sigmoid(gt)
    xg = gt + dtb_ref[...]
    softplus = jnp.maximum(xg, 0.0) + jnp.log1p(jnp.exp(-jnp.abs(xg)))
    g_all = -jnp.exp(alog_ref[...]) * softplus
    row = lax.broadcasted_iota(jnp.int32, (C, C), 0)
    col = lax.broadcasted_iota(jnp.int32, (C, C), 1)
    causal = row >= col
    strict = row > col
    eye = jnp.where(row == col, 1.0, 0.0).astype(F32)
    rr = lax.broadcasted_iota(jnp.int32, (R, R), 0)
    cc = lax.broadcasted_iota(jnp.int32, (R, R), 1)
    trilb = jnp.where(jnp.logical_and(rr >= cc, (rr // C) == (cc // C)), 1.0, 0.0).astype(BF16)
    ghi = g_all.astype(BF16)
    r1 = g_all - ghi.astype(F32)
    gmid = r1.astype(BF16)
    glo = (r1 - gmid.astype(F32)).astype(BF16)
    G_all = _dot(trilb, ghi) + (_dot(trilb, gmid) + _dot(trilb, glo))
    G_t = jnp.transpose(G_all)
    gn = gn_ref[...]

    qn, kn = [], []
    for h in range(GDN_HEADS):
        q = cq[:, h * GDN_HEAD_DIM:(h + 1) * GDN_HEAD_DIM]
        k = cq[:, GDN_K + h * GDN_HEAD_DIM:GDN_K + (h + 1) * GDN_HEAD_DIM]
        qn.append(q * lax.rsqrt(jnp.sum(q * q, axis=-1, keepdims=True) + 1e-6) * (GDN_HEAD_DIM ** -0.5))
        kn.append(k * lax.rsqrt(jnp.sum(k * k, axis=-1, keepdims=True) + 1e-6))

    keys = [(n, h) for n in range(nchunks) for h in range(GDN_HEADS)]
    st = {}
    for (n, h) in keys:
        rs = slice(n * C, (n + 1) * C)
        q = qn[h][rs]
        k = kn[h][rs]
        v = cq[rs, 2 * GDN_K + h * GDN_HEAD_DIM:2 * GDN_K + (h + 1) * GDN_HEAD_DIM]
        beta = beta_all[rs, h:h + 1]
        Gc = G_all[rs, GDN_HEADS + h:GDN_HEADS + h + 1]
        Gr = G_t[GDN_HEADS + h:GDN_HEADS + h + 1, rs]
        Gl = Gc[C - 1:C, :]
        decay = jnp.where(causal, jnp.exp(jnp.where(causal, Gc - Gr, 0.0)), 0.0)
        eG = jnp.exp(Gc)
        kb = k * beta
        st[n, h] = dict(decay=decay, kbf=k.astype(BF16), kkb=kb.astype(BF16), qb=q.astype(BF16),
                        rhs=jnp.concatenate([v * beta, kb * eG], axis=1),
                        qg=(q * eG).astype(BF16), kg=k * jnp.exp(Gl - Gc), gl=jnp.exp(Gl))
    for key in keys:
        d = st[key]
        d["kk"] = _dot(d["kkb"], d["kbf"], _NT)
    for key in keys:
        d = st[key]
        d["qk"] = (_dot(d["qb"], d["kbf"], _NT) * d["decay"]).astype(BF16)
    for key in keys:
        d = st[key]
        nm = jnp.where(strict, d["kk"] * d["decay"], 0.0)
        d["inv"] = eye - nm
        d["pw"] = nm.astype(BF16)
    for _ in range(5):
        for key in keys:
            d = st[key]
            d["pwf"] = _dot(d["pw"], d["pw"])
        for key in keys:
            d = st[key]
            d["pw"] = d["pwf"].astype(BF16)
            d["inv"] = d["inv"] + _dot(d["inv"].astype(BF16), d["pw"])
    for key in keys:
        d = st[key]
        invb = d["inv"].astype(BF16)
        rh, rl = _split_bf16(d["rhs"])
        sol = _dot(invb, rh) + _dot(invb, rl)
        d["u"] = sol[:, :GDN_HEAD_DIM]
        d["w"] = sol[:, GDN_HEAD_DIM:].astype(BF16)
        d["kgt"] = jnp.transpose(d["kg"]).astype(BF16)

    S = [s_sc[h] for h in range(GDN_HEADS)]
    heads = range(GDN_HEADS)
    for n in range(nchunks):
        rs = slice(n * C, (n + 1) * C)
        Sb = [S[h].astype(BF16) for h in heads]
        ws = [_dot(st[n, h]["w"], Sb[h]) for h in heads]
        qs = [_dot(st[n, h]["qg"], Sb[h]) for h in heads]
        vb = [(st[n, h]["u"] - ws[h]).astype(BF16) for h in heads]
        o = [qs[h] + _dot(st[n, h]["qk"], vb[h]) for h in heads]
        for h in heads:
            S[h] = S[h] * st[n, h]["gl"] + _dot(st[n, h]["kgt"], vb[h])
        for h in heads:
            sl = slice(h * GDN_HEAD_DIM, (h + 1) * GDN_HEAD_DIM)
            zz = z_ref[rs, sl]
            og_ref[rs, sl] = (_rms(o[h], gn) * (zz * jax.nn.sigmoid(zz))).astype(og_ref.dtype)
    for h in heads:
        s_sc[h] = S[h]

    @pl.when(c == nc - 1)
    def _():
        sout_ref[...] = s_sc[...]


def _gdn(conv_in, gates, z, conv0p, s0, conv_w, alog_p, dtb_p, gn, nchunks):
    B, L, _ = conv_in.shape
    R = CHUNK * nchunks
    assert L % R == 0
    tok = lambda n: pl.BlockSpec((None, R, n), lambda b, c: (b, c, 0))
    full = lambda a: pl.BlockSpec(a.shape, lambda b, c: (0,) * a.ndim)
    return pl.pallas_call(
        functools.partial(_gdn_kernel, nchunks=nchunks),
        grid=(B, L // R),
        in_specs=[
            tok(GDN_CONV_CH), tok(LANES), tok(GDN_V),
            pl.BlockSpec((None, _PAD, GDN_CONV_CH), lambda b, c: (b, 0, 0)),
            pl.BlockSpec((None, GDN_HEADS, GDN_HEAD_DIM, GDN_HEAD_DIM), lambda b, c: (b, 0, 0, 0)),
            full(conv_w), full(alog_p), full(dtb_p), full(gn),
        ],
        out_specs=(
            tok(GDN_V),
            pl.BlockSpec((None, GDN_HEADS, GDN_HEAD_DIM, GDN_HEAD_DIM), lambda b, c: (b, 0, 0, 0)),
        ),
        out_shape=(
            jax.ShapeDtypeStruct((B, L, GDN_V), BF16),
            jax.ShapeDtypeStruct((B, GDN_HEADS, GDN_HEAD_DIM, GDN_HEAD_DIM), F32),
        ),
        scratch_shapes=[
            pltpu.VMEM((_PAD + R, GDN_CONV_CH), F32),
            pltpu.VMEM((GDN_HEADS, GDN_HEAD_DIM, GDN_HEAD_DIM), F32),
        ],
        compiler_params=pltpu.CompilerParams(dimension_semantics=("arbitrary", "arbitrary"),
                                             vmem_limit_bytes=VMEM_LIMIT),
        name="gdn",
    )(conv_in, gates, z, conv0p, s0, conv_w, alog_p, dtb_p, gn)


def _postmix_kernel(x_ref, oa_ref, og_ref, wo_ref, g2_ref, wr_ref, br_ref, x1_ref, h2_ref, comb_ref):
    mix = _dot(oa_ref[...], wo_ref[0:DA_V, :]) + _dot(og_ref[...], wo_ref[DA_V:, :])
    x1 = x_ref[...] + mix
    x1_ref[...] = x1
    h2 = _rms(x1, g2_ref[...])
    h2_ref[...] = _pack_bf16_pairs(h2)
    logits = _dot3(h2, wr_ref[...]) + br_ref[...]
    lane = lax.broadcasted_iota(jnp.int32, logits.shape, 1).astype(F32)
    work = logits
    sel = jnp.zeros(logits.shape, jnp.bool_)
    top = jnp.max(work, axis=1, keepdims=True)
    for _ in range(TOP_K):
        mx = jnp.max(work, axis=1, keepdims=True)
        first = jnp.min(jnp.where(work == mx, lane, float(LANES)), axis=1, keepdims=True)
        pick = lane == first
        sel = jnp.logical_or(sel, pick)
        work = jnp.where(pick, -jnp.inf, work)
    e = jnp.where(sel, jnp.exp(logits - top), 0.0)
    comb_ref[...] = e / jnp.sum(e, axis=1, keepdims=True) + pltpu.roll(jnp.where(sel, 1.0, 0.0), N_EXPERTS, axis=1)


def _post_mix(x, oa, og, wo, g2, wr, br, tm):
    T = x.shape[0]
    row = lambda n: pl.BlockSpec((tm, n), lambda i: (i, 0))
    full = lambda a: pl.BlockSpec(a.shape, lambda i: (0,) * a.ndim)
    return pl.pallas_call(
        _postmix_kernel,
        grid=(T // tm,),
        in_specs=[row(D_MODEL), row(DA_V), row(GDN_V), full(wo), full(g2), full(wr), full(br)],
        out_specs=(row(D_MODEL), row(D_MODEL // 2), row(LANES)),
        out_shape=(
            jax.ShapeDtypeStruct((T, D_MODEL), F32),
            jax.ShapeDtypeStruct((T, D_MODEL // 2), jnp.uint32),
            jax.ShapeDtypeStruct((T, LANES), F32),
        ),
        compiler_params=pltpu.CompilerParams(dimension_semantics=("arbitrary",), vmem_limit_bytes=VMEM_LIMIT),
        name="post_mix",
    )(x, oa, og, wo, g2, wr, br)


_RT = 512


def _route_kernel(comb_ref, ek_ref, pk_ref, gk_ref, cnt_ref, carry_sc):
    i = pl.program_id(0)

    @pl.when(i == 0)
    def _():
        carry_sc[...] = jnp.zeros(carry_sc.shape, F32)

    comb = comb_ref[...]
    tr = comb.shape[0]
    lane = lax.broadcasted_iota(jnp.int32, comb.shape, 1).astype(F32)
    sel = jnp.where(lane >= float(N_EXPERTS), comb, 0.0)
    r = lax.broadcasted_iota(jnp.int32, (tr, tr), 0)
    c = lax.broadcasted_iota(jnp.int32, (tr, tr), 1)
    below = jnp.where(r > c, 1.0, 0.0).astype(BF16)
    rank = _dot(below, sel.astype(BF16)) + carry_sc[...]
    carry_sc[...] = carry_sc[...] + jnp.sum(sel, axis=0, keepdims=True)
    rest = sel
    ek = jnp.zeros(comb.shape, F32)
    pk = jnp.zeros(comb.shape, F32)
    gk = jnp.zeros(comb.shape, F32)
    for k in range(TOP_K):
        first = jnp.min(jnp.where(rest > 0.0, lane, float(LANES)), axis=1, keepdims=True)
        hit = lane == first
        expert = first - float(N_EXPERTS)
        ek = jnp.where(lane == float(k), expert, ek)
        pk = jnp.where(lane == float(k), jnp.sum(jnp.where(hit, rank, 0.0), axis=1, keepdims=True), pk)
        gk = jnp.where(lane == float(k), jnp.sum(jnp.where(lane == expert, comb, 0.0), axis=1, keepdims=True), gk)
        rest = jnp.where(hit, 0.0, rest)
    ek_ref[...] = ek.astype(jnp.int32)
    pk_ref[...] = pk.astype(jnp.int32)
    gk_ref[...] = gk

    @pl.when(i == pl.num_programs(0) - 1)
    def _():
        cnt_ref[...] = carry_sc[...].astype(jnp.int32)


def _route(comb):
    T = comb.shape[0]
    tr = _pick(T, (_RT, 256, 128, 64))
    row = pl.BlockSpec((tr, LANES), lambda i: (i, 0))
    return pl.pallas_call(
        _route_kernel,
        grid=(T // tr,),
        in_specs=[row],
        out_specs=(row, row, row, pl.BlockSpec((1, LANES), lambda i: (0, 0))),
        out_shape=(
            jax.ShapeDtypeStruct((T, LANES), jnp.int32),
            jax.ShapeDtypeStruct((T, LANES), jnp.int32),
            jax.ShapeDtypeStruct((T, LANES), F32),
            jax.ShapeDtypeStruct((1, LANES), jnp.int32),
        ),
        scratch_shapes=[pltpu.VMEM((1, LANES), F32)],
        compiler_params=pltpu.CompilerParams(dimension_semantics=("arbitrary",)),
        name="route",
    )(comb)


_SC_WIN = 128
_SC_COLS = 128


def _sc_mesh():
    return plsc.VectorSubcoreMesh(core_axis_name="c", subcore_axis_name="s")


def _scatter_rows(x, idx, n_out):
    K, T = idx.shape
    D = x.shape[1]
    assert T % _SC_WIN == 0 and D % _SC_COLS == 0
    flat = idx.reshape(1, K * T)
    nt = T // _SC_WIN

    @pl.kernel(out_type=jax.ShapeDtypeStruct((n_out, D), x.dtype), mesh=_sc_mesh(), scratch_types=[])
    def kern(x_hbm, i_hbm, o_hbm):
        for dc in range(D // _SC_COLS):
            def body(x_vmem, i_vmem, dc=dc):
                pltpu.sync_copy(x_vmem, o_hbm.at[i_vmem.at[0], pl.ds(dc * _SC_COLS, _SC_COLS)])

            pltpu.emit_pipeline(
                body,
                grid=(K * nt,),
                in_specs=[pl.BlockSpec((_SC_WIN, _SC_COLS), lambda i, dc=dc: (i % nt, dc)),
                          pl.BlockSpec((1, _SC_WIN), lambda i: (0, i))],
                out_specs=[],
                core_axis_name=("c", "s"),
                dimension_semantics=(pltpu.PARALLEL,),
            )(x_hbm, i_hbm)

    return kern(x, flat)


def _gather_rows(y, idx):
    K, T = idx.shape
    D = y.shape[1]
    assert (K * T) % _SC_WIN == 0 and D % _SC_COLS == 0
    flat = idx.reshape(1, K * T)

    @pl.kernel(out_type=jax.ShapeDtypeStruct((K * T, D), y.dtype), mesh=_sc_mesh(), scratch_types=[])
    def kern(y_hbm, i_hbm, o_hbm):
        for dc in range(D // _SC_COLS):
            def body(i_vmem, o_vmem, dc=dc):
                pltpu.sync_copy(y_hbm.at[i_vmem.at[0], pl.ds(dc * _SC_COLS, _SC_COLS)], o_vmem)

            pltpu.emit_pipeline(
                body,
                grid=(K * T // _SC_WIN,),
                in_specs=[pl.BlockSpec((1, _SC_WIN), lambda i: (0, i))],
                out_specs=[pl.BlockSpec((_SC_WIN, _SC_COLS), lambda i, dc=dc: (i, dc))],
                core_axis_name=("c", "s"),
                dimension_semantics=(pltpu.PARALLEL,),
            )(i_hbm, o_hbm)

    return kern(y, flat)


_FF_BLK = 512


def _expert_kernel(te_ref, na_ref, x_ref, wg_ref, bg_ref, wu_ref, bu_ref, wd_ref, bd_ref, y_ref):
    n = pl.program_id(0)

    @pl.when(n < na_ref[0])
    def _():
        h = _unpack_bf16_pairs(x_ref[...]).astype(BF16)
        o = bd_ref[...]
        for f in range(D_FF // _FF_BLK):
            fs = slice(f * _FF_BLK, (f + 1) * _FF_BLK)
            gt = jnp.minimum(_dot(h, wg_ref[:, fs]) + bg_ref[:, fs], SWIGLU_LIMIT)
            up = jnp.clip(_dot(h, wu_ref[:, fs]) + bu_ref[:, fs], -SWIGLU_LIMIT, SWIGLU_LIMIT)
            act = (up + 1.0) * gt * jax.nn.sigmoid(SWIGLU_ALPHA * gt)
            o = o + _dot(act.astype(BF16), wd_ref[fs, :])
        y_ref[...] = _pack_bf16_pairs(o)


def _experts(xs, tile_expert, n_active, wg, bg, wu, bu, wd, bd, tmr):
    S = xs.shape[0]
    rows = pl.BlockSpec((tmr, D_MODEL // 2), lambda n, te, na: (jnp.minimum(n, na[0] - 1), 0))
    wspec = pl.BlockSpec((None, D_MODEL, D_FF), lambda n, te, na: (te[n], 0, 0))
    bspec = pl.BlockSpec((None, 1, D_FF), lambda n, te, na: (te[n], 0, 0))
    return pl.pallas_call(
        _expert_kernel,
        grid_spec=pltpu.PrefetchScalarGridSpec(
            num_scalar_prefetch=2,
            grid=(S // tmr,),
            in_specs=[rows, wspec, bspec, wspec, bspec, wspec, bspec],
            out_specs=rows,
        ),
        out_shape=jax.ShapeDtypeStruct((S, D_MODEL // 2), jnp.uint32),
        compiler_params=pltpu.CompilerParams(dimension_semantics=("arbitrary",), vmem_limit_bytes=VMEM_LIMIT),
        name="experts",
    )(tile_expert, n_active, xs, wg, bg, wu, bu, wd, bd)


def _combine_kernel(x1_ref, yg_ref, gk_ref, fg_ref, y_ref):
    gk = gk_ref[...]
    moe = gk[:, 0:1] * _unpack_bf16_pairs(yg_ref[0])
    for k in range(1, TOP_K):
        moe = moe + gk[:, k:k + 1] * _unpack_bf16_pairs(yg_ref[k])
    y_ref[...] = _rms(x1_ref[...] + moe, fg_ref[...])


def _combine(x1, yg, gk, fg, tm):
    T = x1.shape[0]
    row = lambda n: pl.BlockSpec((tm, n), lambda i: (i, 0))
    return pl.pallas_call(
        _combine_kernel,
        grid=(T // tm,),
        in_specs=[row(D_MODEL), pl.BlockSpec((TOP_K, tm, D_MODEL // 2), lambda i: (0, i, 0)), row(LANES),
                  pl.BlockSpec(fg.shape, lambda i: (0, 0))],
        out_specs=row(D_MODEL),
        out_shape=jax.ShapeDtypeStruct((T, D_MODEL), F32),
        compiler_params=pltpu.CompilerParams(dimension_semantics=("arbitrary",), vmem_limit_bytes=VMEM_LIMIT),
        name="combine",
    )(x1, yg, gk, fg)


def _moe(h2, comb, x1, wg, bg, wu, bu, wd, bd, fg, tmr):
    T = h2.shape[0]
    ek, pk, gk, cnt = _route(comb)
    cnt = cnt[0, N_EXPERTS:2 * N_EXPERTS]
    tiles_e = (cnt + tmr - 1) // tmr
    tile_end = jnp.cumsum(tiles_e)
    off = (tile_end - tiles_e) * tmr
    n_tiles = T * TOP_K // tmr + N_EXPERTS
    n_active = tile_end[-1:].astype(jnp.int32)
    tile_expert = jnp.searchsorted(tile_end, jnp.minimum(jnp.arange(n_tiles), n_active[0] - 1), side="right")
    tile_expert = jnp.minimum(tile_expert, N_EXPERTS - 1).astype(jnp.int32)
    slot = (off[ek[:, :TOP_K]] + pk[:, :TOP_K]).astype(jnp.int32).T
    xs = _scatter_rows(h2, slot, n_tiles * tmr)
    ys = _experts(xs, tile_expert, n_active, wg, bg, wu, bu, wd, bd, tmr)
    yg = _gather_rows(ys, slot).reshape(TOP_K, T, D_MODEL // 2)
    return _combine(x1, yg, gk, fg, _pick(T, (256,)))


def _pick(n, prefs):
    for t in prefs:
        if n % t == 0:
            return t
    raise ValueError(f"no tile for {n}")


def _layer(x, k_past, v_past, S0, conv0, lam_init, wts):
    (norm1_g, w_main, w_vt, w_gates, lamv, subg, conv_w, alog_p, dtb_p, gn, wo, g2, wr, br,
     wg, bg, wu, bu, wd, bd) = wts
    B, L, _ = x.shape
    T = B * L
    xt = x.reshape(T, D_MODEL)
    tm = _pick(T, (256,))
    qb, q2, k, kb, v, vb, vt, conv_in, z, gates = _in_proj(xt, norm1_g, w_main, w_vt, w_gates, tm)
    r3 = lambda a: a.reshape(B, L, a.shape[-1])
    if k_past is None:
        oa = _attn_prompt(r3(q2), r3(kb), vt, lamv, subg.reshape(DA_V_DIM, 1), lam_init, _pick(L, (512, 256)))
    else:
        P = k_past.shape[1]
        oa = _attn_sample(r3(qb), r3(kb), r3(vb), k_past.reshape(B, P, DA_QK), v_past.reshape(B, P, DA_V),
                          lamv, subg, lam_init)
    conv_in3 = r3(conv_in)
    conv0p = jnp.concatenate([jnp.zeros((B, _PAD - (GDN_CONV - 1), GDN_CONV_CH), F32), conv0.astype(F32)], axis=1)
    og, s_new = _gdn(conv_in3, r3(gates), r3(z), conv0p, S0.astype(F32), conv_w, alog_p, dtb_p, gn,
                     _pick(L // CHUNK, (4, 2, 1)))
    x1, h2, comb = _post_mix(xt, oa.reshape(T, DA_V), og.reshape(T, GDN_V), wo, g2, wr, br, tm)
    return (x1, h2, comb, k.reshape(B, L, DA_HEADS, 2, DA_HEAD_DIM), v.reshape(B, L, DA_HEADS, DA_V_DIM),
            s_new, conv_in3[:, L - (GDN_CONV - 1):, :])


def kernel(x_prompt, x_sample, cache_k, cache_v, state_ssm, state_conv, norm1_g, w_in, lambda_q1, lambda_k1,
           lambda_q2, lambda_k2, da_subln_g, conv_w, a_log, dt_bias, gdn_norm_g, w_out, norm2_g, w_router,
           b_router, w_gate, b_gate, w_up, b_up, w_down, b_down, final_g):
    depth = w_in.shape[0]
    assert depth == 1, "single-layer trunk"
    l = 0
    lam_init = 0.8 - 0.6 * math.exp(-0.3 * l)
    bp = x_prompt.shape[0]
    pad_lanes = lambda a, fill=0.0: jnp.concatenate(
        [a.astype(F32), jnp.full(a.shape[:-1] + (LANES - a.shape[-1],), fill, F32)], axis=-1)
    w_gates = pad_lanes(w_in[l][:, OFF_B:])
    zeros4 = jnp.zeros((GDN_HEADS,), F32)
    alog_p = pad_lanes(jnp.concatenate([zeros4, a_log[l].astype(F32)])[None, :])
    dtb_p = pad_lanes(jnp.concatenate([zeros4, dt_bias[l].astype(F32)])[None, :])
    wts = (
        norm1_g[l][None, :].astype(F32),
        w_in[l][:, :OFF_B].astype(BF16),
        w_in[l][:, OFF_VA:OFF_CONV].T.astype(BF16),
        w_gates,
        jnp.stack([lambda_q1[l], lambda_k1[l], lambda_q2[l], lambda_k2[l]]).astype(F32),
        da_subln_g[l][None, :].astype(F32),
        conv_w[l].astype(F32),
        alog_p, dtb_p,
        gdn_norm_g[l][None, :].astype(F32),
        w_out[l].astype(BF16),
        norm2_g[l][None, :].astype(F32),
        pad_lanes(w_router[l]),
        pad_lanes(b_router[l][None, :], NEG_INF),
        w_gate[l].astype(BF16), b_gate[l][:, None, :].astype(F32),
        w_up[l].astype(BF16), b_up[l][:, None, :].astype(F32),
        w_down[l].astype(BF16), b_down[l][:, None, :].astype(F32),
    )
    moe_w = wts[14:]
    fg = final_g[None, :].astype(F32)

    s0 = jnp.zeros((bp, GDN_HEADS, GDN_HEAD_DIM, GDN_HEAD_DIM), x_prompt.dtype)
    c0 = jnp.zeros((bp, GDN_CONV - 1, GDN_CONV_CH), x_prompt.dtype)
    x1p, h2p, combp, k1, v1, s1, c1 = _layer(x_prompt, None, None, s0, c0, lam_init, wts)
    x1s, h2s, combs, k2, v2, s2, c2 = _layer(x_sample, cache_k[l], cache_v[l], state_ssm[l], state_conv[l],
                                             lam_init, wts)
    y_p = _moe(h2p, combp, x1p, *moe_w, fg, 512)
    y_s = _moe(h2s, combs, x1s, *moe_w, fg, 128)
    st = lambda a: a[None]
    return (y_p.reshape(x_prompt.shape), y_s.reshape(x_sample.shape), st(k1), st(v1), st(s1), st(c1),
            st(k2), st(v2), st(s2), st(c2))
```

```python
import functools
import math

import jax
import jax.numpy as jnp
from jax import lax
from jax.experimental import pallas as pl
from jax.experimental.pallas import tpu as pltpu
from jax.experimental.pallas import tpu_sc as plsc

F32 = jnp.float32
BF16 = jnp.bfloat16

D_MODEL = 1024
CHUNK = 64
RMS_EPS = 1e-6
NEG_INF = -1e30
DA_HEADS = 4
DA_HEAD_DIM = 64
DA_V_DIM = 128
DA_QK = 512
DA_V = 512
GDN_HEADS = 4
GDN_HEAD_DIM = 128
GDN_K = 512
GDN_V = 512
GDN_CONV = 4
GDN_CONV_CH = 1536
OFF_KA = DA_QK
OFF_VA = 2 * DA_QK
OFF_CONV = 2 * DA_QK + DA_V
OFF_Z = OFF_CONV + GDN_CONV_CH
OFF_B = OFF_Z + GDN_V
N_EXPERTS = 32
TOP_K = 4
D_FF = 1024
SWIGLU_LIMIT = 7.0
SWIGLU_ALPHA = 1.702
LANES = 128
VMEM_LIMIT = 56 * 1024 * 1024
LOG2E = 1.4426950408889634


def _split_bf16(a):
    hi = a.astype(BF16)
    lo = (a - hi.astype(F32)).astype(BF16)
    return hi, lo


def _dot(a, b, dims=(((1,), (0,)), ((), ()))):
    return lax.dot_general(a, b, dims, preferred_element_type=F32)


def _dot3(a, b, dims=(((1,), (0,)), ((), ()))):
    ah, al = _split_bf16(a)
    bh, bl = _split_bf16(b)
    return _dot(ah, bh, dims) + (_dot(ah, bl, dims) + _dot(al, bh, dims))


_NT = (((1,), (1,)), ((), ()))


def _rms(x, g):
    return x * lax.rsqrt(jnp.mean(x * x, axis=-1, keepdims=True) + RMS_EPS) * g


def _pack_bf16_pairs(x):
    n = x.shape[1] // 2
    lo = lax.bitcast_convert_type(x[:, :n].astype(BF16).astype(F32), jnp.uint32)
    hi = lax.bitcast_convert_type(x[:, n:].astype(BF16).astype(F32), jnp.uint32)
    return jnp.bitwise_or(lax.shift_right_logical(lo, jnp.uint32(16)), hi)


def _unpack_bf16_pairs(w):
    lo = lax.bitcast_convert_type(lax.shift_left(w, jnp.uint32(16)), F32)
    hi = lax.bitcast_convert_type(jnp.bitwise_and(w, jnp.uint32(0xFFFF0000)), F32)
    return jnp.concatenate([lo, hi], axis=1)


_VB = 256
_VR = DA_V_DIM + 16


def _inproj_kernel(x_ref, g_ref, w_ref, wvt_ref, wg_ref,
                   qb_ref, q2_ref, k_ref, kb_ref, v_ref, vb_ref, vt_ref, cv_ref, z_ref, gt_ref):
    h = _rms(x_ref[...], g_ref[...])
    hb = h.astype(BF16)
    q = _dot(hb, w_ref[:, 0:OFF_KA])
    qb_ref[...] = (q * (DA_HEAD_DIM ** -0.5)).astype(BF16)
    q2_ref[...] = (q * (DA_HEAD_DIM ** -0.5 * LOG2E)).astype(BF16)
    k = _dot(hb, w_ref[:, OFF_KA:OFF_VA])
    k_ref[...] = k
    kb_ref[...] = k.astype(BF16)
    v = _dot(hb, w_ref[:, OFF_VA:OFF_CONV])
    v_ref[...] = v
    vb_ref[...] = v.astype(BF16)
    vt = _dot(wvt_ref[...], hb, _NT).astype(BF16)
    sub = lax.broadcasted_iota(jnp.int32, (_VR - DA_V_DIM, _VB), 0)
    ones_row = jnp.where(sub == 0, 1.0, 0.0).astype(BF16)
    for hd in range(DA_HEADS):
        for t in range(vt.shape[1] // _VB):
            vt_ref[hd, t, 0:DA_V_DIM, :] = vt[hd * DA_V_DIM:(hd + 1) * DA_V_DIM, t * _VB:(t + 1) * _VB]
            vt_ref[hd, t, DA_V_DIM:_VR, :] = ones_row
    for c in range(GDN_CONV_CH // 512):
        cv_ref[:, c * 512:(c + 1) * 512] = _dot(hb, w_ref[:, OFF_CONV + c * 512:OFF_CONV + (c + 1) * 512])
    z_ref[...] = _dot(hb, w_ref[:, OFF_Z:OFF_B])
    gt_ref[...] = _dot3(h, wg_ref[...])


def _in_proj(x, norm1_g, w_main, w_vt, w_gates, tm):
    T = x.shape[0]
    assert tm % _VB == 0
    row = lambda n: pl.BlockSpec((tm, n), lambda i: (i, 0))
    full = lambda a: pl.BlockSpec(a.shape, lambda i: (0,) * a.ndim)
    out_shape = (
        jax.ShapeDtypeStruct((T, DA_QK), BF16),
        jax.ShapeDtypeStruct((T, DA_QK), BF16),
        jax.ShapeDtypeStruct((T, DA_QK), F32),
        jax.ShapeDtypeStruct((T, DA_QK), BF16),
        jax.ShapeDtypeStruct((T, DA_V), F32),
        jax.ShapeDtypeStruct((T, DA_V), BF16),
        jax.ShapeDtypeStruct((DA_HEADS, T // _VB, _VR, _VB), BF16),
        jax.ShapeDtypeStruct((T, GDN_CONV_CH), F32),
        jax.ShapeDtypeStruct((T, GDN_V), F32),
        jax.ShapeDtypeStruct((T, LANES), F32),
    )
    out_specs = tuple(
        pl.BlockSpec((DA_HEADS, tm // _VB, _VR, _VB), lambda i: (0, i, 0, 0)) if len(s.shape) == 4
        else row(s.shape[1]) for s in out_shape)
    return pl.pallas_call(
        _inproj_kernel,
        grid=(T // tm,),
        in_specs=[row(D_MODEL), full(norm1_g), full(w_main), full(w_vt), full(w_gates)],
        out_specs=out_specs,
        out_shape=out_shape,
        compiler_params=pltpu.CompilerParams(dimension_semantics=("arbitrary",), vmem_limit_bytes=VMEM_LIMIT),
        name="in_proj",
    )(x, norm1_g, w_main, w_vt, w_gates)


def _lam_from(lamv_ref, lam_init):
    lv = lamv_ref[...]
    s1 = jnp.sum(lv[0:1, :] * lv[1:2, :], axis=-1, keepdims=True)
    s2 = jnp.sum(lv[2:3, :] * lv[3:4, :], axis=-1, keepdims=True)
    return jnp.exp(s1) - jnp.exp(s2) + lam_init


def _stack_maps(q):
    lane = lax.broadcasted_iota(jnp.int32, q.shape, 1)
    zero = jnp.zeros_like(q)
    return jnp.concatenate([jnp.where(lane < DA_HEAD_DIM, q, zero), jnp.where(lane >= DA_HEAD_DIM, q, zero)], axis=0)


def _attn_finish(acc, l, lam, subg, lam_init, tq):
    o = acc[:tq] / l[:tq] - lam * (acc[tq:] / l[tq:])
    return _rms(o, subg) * (1.0 - lam_init)


_CT = 256


def _attn_prompt_kernel(lamv_ref, subg_ref, q_ref, k_ref, vt_ref, o_ref, qs_sc, m_sc, acc_sc, s_sc, pb_sc,
                        al_sc, *, tq, tk, lam_init):
    i = pl.program_id(2)
    nq = 2 * tq
    nct = nq // _CT
    nvt = tk // _VB
    kpq = tq // tk
    qt = jnp.transpose(q_ref[...].astype(F32))
    d = lax.broadcasted_iota(jnp.int32, qt.shape, 0)
    qs_sc[:, 0:tq] = jnp.where(d < DA_HEAD_DIM, qt, 0.0).astype(BF16)
    qs_sc[:, tq:nq] = jnp.where(d >= DA_HEAD_DIM, qt, 0.0).astype(BF16)
    m_sc[...] = jnp.full(m_sc.shape, NEG_INF, F32)
    acc_sc[...] = jnp.zeros(acc_sc.shape, F32)

    def kblock(j):
        return k_ref[pl.ds(pl.multiple_of(j * tk, tk), tk), :]

    def qk(kb, c):
        return _dot(kb, qs_sc[:, c * _CT:(c + 1) * _CT])

    def soft(c, s, rel):
        cs = slice(c * _CT, (c + 1) * _CT)
        if rel is not None:
            key = lax.broadcasted_iota(jnp.int32, s.shape, 0) + rel
            qry = lax.broadcasted_iota(jnp.int32, s.shape, 1) + ((c * _CT) % tq)
            s = jnp.where((qry // CHUNK) >= (key // CHUNK), s, NEG_INF)
        m_prev = m_sc[:, cs]
        m_new = jnp.maximum(m_prev, jnp.max(s, axis=0, keepdims=True))
        alpha = jnp.exp2(m_prev - m_new)
        p = jnp.exp2(s - m_new)
        m_sc[:, cs] = m_new
        return p.astype(BF16), alpha

    def pv(j, c, pb, alpha):
        cs = slice(c * _CT, (c + 1) * _CT)
        upd = _dot(vt_ref[j * nvt], pb[0:_VB, :])
        for t in range(1, nvt):
            upd = upd + _dot(vt_ref[j * nvt + t], pb[t * _VB:(t + 1) * _VB, :])
        acc_sc[:, cs] = alpha * acc_sc[:, cs] + upd

    s_sc[...] = qk(kblock(0), 0)
    pb_sc[...] = jnp.zeros(pb_sc.shape, BF16)
    al_sc[...] = jnp.ones(al_sc.shape, F32)

    def run(steps, j_prev, c_prev, nxt):
        s_cur = s_sc[...]
        pend = None
        for n, (j, c, rel) in enumerate(steps):
            if n + 1 < len(steps):
                s_nxt = qk(kblock(steps[n + 1][0]), steps[n + 1][1])
            elif nxt is not None:
                s_sc[...] = qk(kblock(nxt[0]), nxt[1])
            pb, alpha = soft(c, s_cur, rel)
            if n == 0:
                pv(j_prev, c_prev, pb_sc[...], al_sc[...])
            else:
                pv(steps[n - 1][0], steps[n - 1][1], *pend)
            pend = (pb, alpha)
            if n + 1 < len(steps):
                s_cur = s_nxt
        pb_sc[...] = pend[0]
        al_sc[...] = pend[1]

    def body(j, carry):
        run([(j, c, None) for c in range(nct)], jnp.maximum(j - 1, 0), nct - 1, (j + 1, 0))
        return carry

    j0 = i * kpq
    lax.fori_loop(0, j0, body, 0)
    diag = []
    for r in range(kpq):
        for c in range(nct):
            q_lo = (c * _CT) % tq
            if q_lo + _CT <= r * tk:
                continue
            diag.append((j0 + r, c, None if q_lo >= (r + 1) * tk else r * tk))
    run(diag, jnp.maximum(j0 - 1, 0), nct - 1, None)
    pv(diag[-1][0], diag[-1][1], pb_sc[...], al_sc[...])
    lam = _lam_from(lamv_ref, lam_init)
    acc = acc_sc[0:DA_V_DIM, :]
    l = acc_sc[DA_V_DIM:DA_V_DIM + 1, :]
    ot = acc[:, 0:tq] / l[:, 0:tq] - lam * (acc[:, tq:nq] / l[:, tq:nq])
    ot = ot * lax.rsqrt(jnp.mean(ot * ot, axis=0, keepdims=True) + RMS_EPS) * subg_ref[...] * (1.0 - lam_init)
    o_ref[...] = jnp.transpose(ot).astype(o_ref.dtype)


def _attn_prompt(qb, kb, vt, lamv, subg_col, lam_init, tq, tk):
    B, L, _ = qb.shape
    assert tk % _VB == 0 and (2 * tq) % _CT == 0 and tq % tk == 0 and L % tq == 0 and _CT <= tk
    kern = functools.partial(_attn_prompt_kernel, tq=tq, tk=tk, lam_init=lam_init)
    nvb = L // _VB
    return pl.pallas_call(
        kern,
        grid=(B, DA_HEADS, L // tq),
        in_specs=[
            pl.BlockSpec(lamv.shape, lambda b, h, i: (0, 0)),
            pl.BlockSpec(subg_col.shape, lambda b, h, i: (0, 0)),
            pl.BlockSpec((None, tq, LANES), lambda b, h, i: (b, i, h)),
            pl.BlockSpec((None, L, LANES), lambda b, h, i: (b, 0, h)),
            pl.BlockSpec((None, nvb, _VR, _VB), lambda b, h, i: (h, b, 0, 0)),
        ],
        out_specs=pl.BlockSpec((None, tq, LANES), lambda b, h, i: (b, i, h)),
        out_shape=jax.ShapeDtypeStruct((B, L, DA_V), BF16),
        scratch_shapes=[
            pltpu.VMEM((LANES, 2 * tq), BF16),
            pltpu.VMEM((1, 2 * tq), F32),
            pltpu.VMEM((_VR, 2 * tq), F32),
            pltpu.VMEM((tk, _CT), F32),
            pltpu.VMEM((tk, _CT), BF16),
            pltpu.VMEM((1, _CT), F32),
        ],
        compiler_params=pltpu.CompilerParams(
            dimension_semantics=("arbitrary", "arbitrary", "arbitrary"), vmem_limit_bytes=VMEM_LIMIT),
        name="attn_prompt",
    )(lamv, subg_col, qb, kb, vt)


def _attn_sample_kernel(lamv_ref, subg_ref, q_ref, kp_ref, vp_ref, kn_ref, vn_ref, o_ref, *, lam_init):
    tq = q_ref.shape[0]
    qs = _stack_maps(q_ref[...])
    sp = _dot(qs, kp_ref[...].astype(BF16), _NT)
    sn = _dot(qs, kn_ref[...], _NT)
    m = jnp.maximum(jnp.max(sp, axis=1, keepdims=True), jnp.max(sn, axis=1, keepdims=True))
    pp = jnp.exp(sp - m)
    pn = jnp.exp(sn - m)
    l = jnp.sum(pp, axis=1, keepdims=True) + jnp.sum(pn, axis=1, keepdims=True)
    acc = _dot(pp.astype(BF16), vp_ref[...].astype(BF16)) + _dot(pn.astype(BF16), vn_ref[...])
    lam = _lam_from(lamv_ref, lam_init)
    o_ref[...] = _attn_finish(acc, l, lam, subg_ref[...], lam_init, tq).astype(o_ref.dtype)


def _attn_sample(qb, kb, vb, cache_k, cache_v, lamv, subg, lam_init):
    B, L, _ = qb.shape
    P = cache_k.shape[1]
    kern = functools.partial(_attn_sample_kernel, lam_init=lam_init)
    new = pl.BlockSpec((None, L, LANES), lambda b, h: (b, 0, h))
    past = pl.BlockSpec((None, P, LANES), lambda b, h: (b, 0, h))
    return pl.pallas_call(
        kern,
        grid=(B, DA_HEADS),
        in_specs=[
            pl.BlockSpec(lamv.shape, lambda b, h: (0, 0)),
            pl.BlockSpec(subg.shape, lambda b, h: (0, 0)),
            new, past, past, new, new,
        ],
        out_specs=new,
        out_shape=jax.ShapeDtypeStruct((B, L, DA_V), BF16),
        compiler_params=pltpu.CompilerParams(dimension_semantics=("arbitrary", "arbitrary")),
        name="attn_sample",
    )(lamv, subg, qb, cache_k, cache_v, kb, vb)


_PAD = 8


def _gdn_kernel(u_ref, gt_ref, z_ref, c0_ref, s0_ref, cw_ref, alog_ref, dtb_ref, gn_ref,
                og_ref, sout_ref, ubuf, s_sc, *, nchunks):
    c = pl.program_id(1)
    nc = pl.num_programs(1)
    C = CHUNK
    R = C * nchunks

    @pl.when(c == 0)
    def _():
        ubuf[0:_PAD, :] = c0_ref[...]
        s_sc[...] = s0_ref[...]

    u = u_ref[...]
    ubuf[_PAD:_PAD + R, :] = u
    cw = cw_ref[...]
    y = u * cw[3:4, :]
    for j in range(1, GDN_CONV):
        y = y + ubuf[_PAD - j:_PAD - j + R, :] * cw[3 - j:4 - j, :]
    ubuf[0:_PAD, :] = u[R - _PAD:, :]
    cq = y * jax.nn.sigmoid(y)

    gt = gt_ref[...]
    beta_all = jax.nn.sigmoid(gt)
    xg = gt + dtb_ref[...]
    softplus = jnp.maximum(xg, 0.0) + jnp.log1p(jnp.exp(-jnp.abs(xg)))
    g_all = -jnp.exp(alog_ref[...]) * softplus
    row = lax.broadcasted_iota(jnp.int32, (C, C), 0)
    col = lax.broadcasted_iota(jnp.int32, (C, C), 1)
    causal = row >= col
    strict = row > col
    eye = jnp.where(row == col, 1.0, 0.0).astype(F32)
    rr = lax.broadcasted_iota(jnp.int32, (R, R), 0)
    cc = lax.broadcasted_iota(jnp.int32, (R, R), 1)
    trilb = jnp.where(jnp.logical_and(rr >= cc, (rr // C) == (cc // C)), 1.0, 0.0).astype(BF16)
    ghi = g_all.astype(BF16)
    r1 = g_all - ghi.astype(F32)
    gmid = r1.astype(BF16)
    glo = (r1 - gmid.astype(F32)).astype(BF16)
    G_all = _dot(trilb, ghi) + (_dot(trilb, gmid) + _dot(trilb, glo))
    G_t = jnp.transpose(G_all)
    gn = gn_ref[...]

    qn, kn = [], []
    for h in range(GDN_HEADS):
        q = cq[:, h * GDN_HEAD_DIM:(h + 1) * GDN_HEAD_DIM]
        k = cq[:, GDN_K + h * GDN_HEAD_DIM:GDN_K + (h + 1) * GDN_HEAD_DIM]
        qn.append(q * lax.rsqrt(jnp.sum(q * q, axis=-1, keepdims=True) + 1e-6) * (GDN_HEAD_DIM ** -0.5))
        kn.append(k * lax.rsqrt(jnp.sum(k * k, axis=-1, keepdims=True) + 1e-6))

    keys = [(n, h) for n in range(nchunks) for h in range(GDN_HEADS)]
    st = {}
    for (n, h) in keys:
        rs = slice(n * C, (n + 1) * C)
        q = qn[h][rs]
        k = kn[h][rs]
        v = cq[rs, 2 * GDN_K + h * GDN_HEAD_DIM:2 * GDN_K + (h + 1) * GDN_HEAD_DIM]
        beta = beta_all[rs, h:h + 1]
        Gc = G_all[rs, GDN_HEADS + h:GDN_HEADS + h + 1]
        Gr = G_t[GDN_HEADS + h:GDN_HEADS + h + 1, rs]
        Gl = Gc[C - 1:C, :]
        decay = jnp.where(causal, jnp.exp(jnp.where(causal, Gc - Gr, 0.0)), 0.0)
        eG = jnp.exp(Gc)
        kb = k * beta
        st[n, h] = dict(decay=decay, kbf=k.astype(BF16), kkb=kb.astype(BF16), qb=q.astype(BF16),
                        rhs=jnp.concatenate([v * beta, kb * eG], axis=1),
                        qg=(q * eG).astype(BF16), kg=k * jnp.exp(Gl - Gc), gl=jnp.exp(Gl))
    for key in keys:
        d = st[key]
        d["kk"] = _dot(d["kkb"], d["kbf"], _NT)
    for key in keys:
        d = st[key]
        d["qk"] = (_dot(d["qb"], d["kbf"], _NT) * d["decay"]).astype(BF16)
    for key in keys:
        d = st[key]
        nm = jnp.where(strict, d["kk"] * d["decay"], 0.0)
        d["inv"] = eye - nm
        d["pw"] = nm.astype(BF16)
    for _ in range(5):
        for key in keys:
            d = st[key]
            d["pwf"] = _dot(d["pw"], d["pw"])
        for key in keys:
            d = st[key]
            d["pw"] = d["pwf"].astype(BF16)
            d["inv"] = d["inv"] + _dot(d["inv"].astype(BF16), d["pw"])
    for key in keys:
        d = st[key]
        invb = d["inv"].astype(BF16)
        rh, rl = _split_bf16(d["rhs"])
        sol = _dot(invb, rh) + _dot(invb, rl)
        d["u"] = sol[:, :GDN_HEAD_DIM]
        d["w"] = sol[:, GDN_HEAD_DIM:].astype(BF16)
        d["kgt"] = jnp.transpose(d["kg"]).astype(BF16)

    S = [s_sc[h] for h in range(GDN_HEADS)]
    heads = range(GDN_HEADS)
    for n in range(nchunks):
        rs = slice(n * C, (n + 1) * C)
        Sb = [S[h].astype(BF16) for h in heads]
        ws = [_dot(st[n, h]["w"], Sb[h]) for h in heads]
        qs = [_dot(st[n, h]["qg"], Sb[h]) for h in heads]
        vb = [(st[n, h]["u"] - ws[h]).astype(BF16) for h in heads]
        o = [qs[h] + _dot(st[n, h]["qk"], vb[h]) for h in heads]
        for h in heads:
            S[h] = S[h] * st[n, h]["gl"] + _dot(st[n, h]["kgt"], vb[h])
        for h in heads:
            sl = slice(h * GDN_HEAD_DIM, (h + 1) * GDN_HEAD_DIM)
            zz = z_ref[rs, sl]
            og_ref[rs, sl] = (_rms(o[h], gn) * (zz * jax.nn.sigmoid(zz))).astype(og_ref.dtype)
    for h in heads:
        s_sc[h] = S[h]

    @pl.when(c == nc - 1)
    def _():
        sout_ref[...] = s_sc[...]


def _gdn(conv_in, gates, z, conv0p, s0, conv_w, alog_p, dtb_p, gn, nchunks):
    B, L, _ = conv_in.shape
    R = CHUNK * nchunks
    assert L % R == 0
    tok = lambda n: pl.BlockSpec((None, R, n), lambda b, c: (b, c, 0))
    full = lambda a: pl.BlockSpec(a.shape, lambda b, c: (0,) * a.ndim)
    return pl.pallas_call(
        functools.partial(_gdn_kernel, nchunks=nchunks),
        grid=(B, L // R),
        in_specs=[
            tok(GDN_CONV_CH), tok(LANES), tok(GDN_V),
            pl.BlockSpec((None, _PAD, GDN_CONV_CH), lambda b, c: (b, 0, 0)),
            pl.BlockSpec((None, GDN_HEADS, GDN_HEAD_DIM, GDN_HEAD_DIM), lambda b, c: (b, 0, 0, 0)),
            full(conv_w), full(alog_p), full(dtb_p), full(gn),
        ],
        out_specs=(
            tok(GDN_V),
            pl.BlockSpec((None, GDN_HEADS, GDN_HEAD_DIM, GDN_HEAD_DIM), lambda b, c: (b, 0, 0, 0)),
        ),
        out_shape=(
            jax.ShapeDtypeStruct((B, L, GDN_V), BF16),
            jax.ShapeDtypeStruct((B, GDN_HEADS, GDN_HEAD_DIM, GDN_HEAD_DIM), F32),
        ),
        scratch_shapes=[
            pltpu.VMEM((_PAD + R, GDN_CONV_CH), F32),
            pltpu.VMEM((GDN_HEADS, GDN_HEAD_DIM, GDN_HEAD_DIM), F32),
        ],
        compiler_params=pltpu.CompilerParams(dimension_semantics=("arbitrary", "arbitrary"),
                                             vmem_limit_bytes=VMEM_LIMIT),
        name="gdn",
    )(conv_in, gates, z, conv0p, s0, conv_w, alog_p, dtb_p, gn)


def _postmix_kernel(x_ref, oa_ref, og_ref, wo_ref, g2_ref, wr_ref, br_ref, x1_ref, h2_ref, comb_ref, cnt_ref, cnt_sc):
    i = pl.program_id(0)

    @pl.when(i == 0)
    def _():
        cnt_sc[...] = jnp.zeros(cnt_sc.shape, F32)

    mix = _dot(oa_ref[...], wo_ref[0:DA_V, :]) + _dot(og_ref[...], wo_ref[DA_V:, :])
    x1 = x_ref[...] + mix
    x1_ref[...] = x1
    h2 = _rms(x1, g2_ref[...])
    h2_ref[...] = _pack_bf16_pairs(h2)
    logits = _dot3(h2, wr_ref[...]) + br_ref[...]
    lane = lax.broadcasted_iota(jnp.int32, logits.shape, 1).astype(F32)
    work = logits
    sel = jnp.zeros(logits.shape, jnp.bool_)
    top = jnp.max(work, axis=1, keepdims=True)
    for _ in range(TOP_K):
        mx = jnp.max(work, axis=1, keepdims=True)
        first = jnp.min(jnp.where(work == mx, lane, float(LANES)), axis=1, keepdims=True)
        pick = lane == first
        sel = jnp.logical_or(sel, pick)
        work = jnp.where(pick, -jnp.inf, work)
    e = jnp.where(sel, jnp.exp(logits - top), 0.0)
    flags = pltpu.roll(jnp.where(sel, 1.0, 0.0), N_EXPERTS, axis=1)
    comb_ref[...] = e / jnp.sum(e, axis=1, keepdims=True) + flags
    cnt_sc[...] = cnt_sc[...] + jnp.sum(flags, axis=0, keepdims=True)
    cnt_ref[...] = cnt_sc[...]


def _post_mix(x, oa, og, wo, g2, wr, br, tm):
    T = x.shape[0]
    row = lambda n: pl.BlockSpec((tm, n), lambda i: (i, 0))
    full = lambda a: pl.BlockSpec(a.shape, lambda i: (0,) * a.ndim)
    return pl.pallas_call(
        _postmix_kernel,
        grid=(T // tm,),
        in_specs=[row(D_MODEL), row(DA_V), row(GDN_V), full(wo), full(g2), full(wr), full(br)],
        out_specs=(row(D_MODEL), row(D_MODEL // 2), row(LANES), pl.BlockSpec((1, LANES), lambda i: (0, 0))),
        out_shape=(
            jax.ShapeDtypeStruct((T, D_MODEL), F32),
            jax.ShapeDtypeStruct((T, D_MODEL // 2), jnp.uint32),
            jax.ShapeDtypeStruct((T, LANES), F32),
            jax.ShapeDtypeStruct((1, LANES), F32),
        ),
        scratch_shapes=[pltpu.VMEM((1, LANES), F32)],
        compiler_params=pltpu.CompilerParams(dimension_semantics=("arbitrary",), vmem_limit_bytes=VMEM_LIMIT),
        name="post_mix",
    )(x, oa, og, wo, g2, wr, br)


_RT = 512


def _route_kernel(comb_ref, off_ref, slot_ref, gk_ref, carry_sc):
    i = pl.program_id(0)

    @pl.when(i == 0)
    def _():
        carry_sc[...] = jnp.zeros(carry_sc.shape, F32)

    comb = comb_ref[...]
    tr = comb.shape[0]
    lane = lax.broadcasted_iota(jnp.int32, comb.shape, 1).astype(F32)
    sel = jnp.where(lane >= float(N_EXPERTS), comb, 0.0)
    r = lax.broadcasted_iota(jnp.int32, (tr, tr), 0)
    c = lax.broadcasted_iota(jnp.int32, (tr, tr), 1)
    below = jnp.where(r > c, 1.0, 0.0).astype(BF16)
    pos = _dot(below, sel.astype(BF16)) + (carry_sc[...] + off_ref[...])
    carry_sc[...] = carry_sc[...] + jnp.sum(sel, axis=0, keepdims=True)
    rest = sel
    slot = jnp.zeros(comb.shape, F32)
    gk = jnp.zeros(comb.shape, F32)
    for k in range(TOP_K):
        first = jnp.min(jnp.where(rest > 0.0, lane, float(LANES)), axis=1, keepdims=True)
        hit = lane == first
        expert = first - float(N_EXPERTS)
        slot = jnp.where(lane == float(k), jnp.sum(jnp.where(hit, pos, 0.0), axis=1, keepdims=True), slot)
        gk = jnp.where(lane == float(k), jnp.sum(jnp.where(lane == expert, comb, 0.0), axis=1, keepdims=True), gk)
        rest = jnp.where(hit, 0.0, rest)
    slot_ref[...] = slot.astype(jnp.int32)
    gk_ref[...] = gk


def _route(comb, off_lanes):
    T = comb.shape[0]
    tr = _pick(T, (_RT, 256, 128, 64))
    row = pl.BlockSpec((tr, LANES), lambda i: (i, 0))
    return pl.pallas_call(
        _route_kernel,
        grid=(T // tr,),
        in_specs=[row, pl.BlockSpec((1, LANES), lambda i: (0, 0))],
        out_specs=(row, row),
        out_shape=(
            jax.ShapeDtypeStruct((T, LANES), jnp.int32),
            jax.ShapeDtypeStruct((T, LANES), F32),
        ),
        scratch_shapes=[pltpu.VMEM((1, LANES), F32)],
        compiler_params=pltpu.CompilerParams(dimension_semantics=("arbitrary",)),
        name="route",
    )(comb, off_lanes)


_SC_WIN = 128
_SC_COLS = 128


def _sc_mesh():
    return plsc.VectorSubcoreMesh(core_axis_name="c", subcore_axis_name="s")


def _scatter_rows(x, idx, n_out):
    K, T = idx.shape
    D = x.shape[1]
    assert T % _SC_WIN == 0 and D % _SC_COLS == 0
    flat = idx.reshape(1, K * T)
    nt = T // _SC_WIN

    @pl.kernel(out_type=jax.ShapeDtypeStruct((n_out, D), x.dtype), mesh=_sc_mesh(), scratch_types=[])
    def kern(x_hbm, i_hbm, o_hbm):
        for dc in range(D // _SC_COLS):
            def body(x_vmem, i_vmem, dc=dc):
                pltpu.sync_copy(x_vmem, o_hbm.at[i_vmem.at[0], pl.ds(dc * _SC_COLS, _SC_COLS)])

            pltpu.emit_pipeline(
                body,
                grid=(K * nt,),
                in_specs=[pl.BlockSpec((_SC_WIN, _SC_COLS), lambda i, dc=dc: (i % nt, dc)),
                          pl.BlockSpec((1, _SC_WIN), lambda i: (0, i))],
                out_specs=[],
                core_axis_name=("c", "s"),
                dimension_semantics=(pltpu.PARALLEL,),
            )(x_hbm, i_hbm)

    return kern(x, flat)


def _gather_rows(y, idx):
    K, T = idx.shape
    D = y.shape[1]
    assert (K * T) % _SC_WIN == 0 and D % _SC_COLS == 0
    flat = idx.reshape(1, K * T)

    @pl.kernel(out_type=jax.ShapeDtypeStruct((K * T, D), y.dtype), mesh=_sc_mesh(), scratch_types=[])
    def kern(y_hbm, i_hbm, o_hbm):
        for dc in range(D // _SC_COLS):
            def body(i_vmem, o_vmem, dc=dc):
                pltpu.sync_copy(y_hbm.at[i_vmem.at[0], pl.ds(dc * _SC_COLS, _SC_COLS)], o_vmem)

            pltpu.emit_pipeline(
                body,
                grid=(K * T // _SC_WIN,),
                in_specs=[pl.BlockSpec((1, _SC_WIN), lambda i: (0, i))],
                out_specs=[pl.BlockSpec((_SC_WIN, _SC_COLS), lambda i, dc=dc: (i, dc))],
                core_axis_name=("c", "s"),
                dimension_semantics=(pltpu.PARALLEL,),
            )(i_hbm, o_hbm)

    return kern(y, flat)


_FF_BLK = 512


def _expert_kernel(te_ref, na_ref, x_ref, wg_ref, bg_ref, wu_ref, bu_ref, wd_ref, bd_ref, y_ref, wg_sc, wu_sc, wd_sc):
    n = pl.program_id(0)

    @pl.when(jnp.logical_or(n == 0, te_ref[n] != te_ref[jnp.maximum(n - 1, 0)]))
    def _():
        wg_sc[...] = wg_ref[...].astype(BF16)
        wu_sc[...] = wu_ref[...].astype(BF16)
        wd_sc[...] = wd_ref[...].astype(BF16)

    @pl.when(n < na_ref[0])
    def _():
        h = _unpack_bf16_pairs(x_ref[...]).astype(BF16)
        o = bd_ref[...]
        for f in range(D_FF // _FF_BLK):
            fs = slice(f * _FF_BLK, (f + 1) * _FF_BLK)
            gt = jnp.minimum(_dot(h, wg_sc[:, fs]) + bg_ref[:, fs], SWIGLU_LIMIT)
            up = jnp.clip(_dot(h, wu_sc[:, fs]) + bu_ref[:, fs], -SWIGLU_LIMIT, SWIGLU_LIMIT)
            act = (up + 1.0) * gt * jax.nn.sigmoid(SWIGLU_ALPHA * gt)
            o = o + _dot(act.astype(BF16), wd_sc[fs, :])
        y_ref[...] = _pack_bf16_pairs(o)


def _experts(xs, tile_expert, n_active, wg, bg, wu, bu, wd, bd, tmr):
    S = xs.shape[0]
    rows = pl.BlockSpec((tmr, D_MODEL // 2), lambda n, te, na: (jnp.minimum(n, na[0] - 1), 0))
    wspec = pl.BlockSpec((None, D_MODEL, D_FF), lambda n, te, na: (te[n], 0, 0))
    bspec = pl.BlockSpec((None, 1, D_FF), lambda n, te, na: (te[n], 0, 0))
    return pl.pallas_call(
        _expert_kernel,
        grid_spec=pltpu.PrefetchScalarGridSpec(
            num_scalar_prefetch=2,
            grid=(S // tmr,),
            in_specs=[rows, wspec, bspec, wspec, bspec, wspec, bspec],
            out_specs=rows,
            scratch_shapes=[pltpu.VMEM((D_MODEL, D_FF), BF16), pltpu.VMEM((D_MODEL, D_FF), BF16),
                            pltpu.VMEM((D_FF, D_MODEL), BF16)],
        ),
        out_shape=jax.ShapeDtypeStruct((S, D_MODEL // 2), jnp.uint32),
        compiler_params=pltpu.CompilerParams(dimension_semantics=("arbitrary",), vmem_limit_bytes=VMEM_LIMIT),
        name="experts",
    )(tile_expert, n_active, xs, wg, bg, wu, bu, wd, bd)


def _combine_kernel(x1_ref, yg_ref, gk_ref, fg_ref, y_ref):
    gk = gk_ref[...]
    moe = gk[:, 0:1] * _unpack_bf16_pairs(yg_ref[0])
    for k in range(1, TOP_K):
        moe = moe + gk[:, k:k + 1] * _unpack_bf16_pairs(yg_ref[k])
    y_ref[...] = _rms(x1_ref[...] + moe, fg_ref[...])


def _combine(x1, yg, gk, fg, tm):
    T = x1.shape[0]
    row = lambda n: pl.BlockSpec((tm, n), lambda i: (i, 0))
    return pl.pallas_call(
        _combine_kernel,
        grid=(T // tm,),
        in_specs=[row(D_MODEL), pl.BlockSpec((TOP_K, tm, D_MODEL // 2), lambda i: (0, i, 0)), row(LANES),
                  pl.BlockSpec(fg.shape, lambda i: (0, 0))],
        out_specs=row(D_MODEL),
        out_shape=jax.ShapeDtypeStruct((T, D_MODEL), F32),
        compiler_params=pltpu.CompilerParams(dimension_semantics=("arbitrary",), vmem_limit_bytes=VMEM_LIMIT),
        name="combine",
    )(x1, yg, gk, fg)


def _moe(h2, comb, cnt, x1, wg, bg, wu, bu, wd, bd, fg, tmr):
    T = h2.shape[0]
    cnt_e = cnt[0, N_EXPERTS:2 * N_EXPERTS].astype(jnp.int32)
    tiles_e = (cnt_e + tmr - 1) // tmr
    tile_end = jnp.cumsum(tiles_e)
    off = (tile_end - tiles_e) * tmr
    n_tiles = T * TOP_K // tmr + N_EXPERTS
    n_active = tile_end[-1:].astype(jnp.int32)
    tile_id = jnp.minimum(jnp.arange(n_tiles, dtype=jnp.int32), n_active[0] - 1)
    tile_expert = jnp.sum((tile_end[None, :] <= tile_id[:, None]).astype(jnp.int32), axis=1)
    tile_expert = jnp.minimum(tile_expert, N_EXPERTS - 1).astype(jnp.int32)
    off_lanes = jnp.zeros((1, LANES), F32).at[0, N_EXPERTS:2 * N_EXPERTS].set(off.astype(F32))
    slot128, gk = _route(comb, off_lanes)
    slot = slot128[:, :TOP_K].T
    xs = _scatter_rows(h2, slot, n_tiles * tmr)
    ys = _experts(xs, tile_expert, n_active, wg, bg, wu, bu, wd, bd, tmr)
    yg = _gather_rows(ys, slot).reshape(TOP_K, T, D_MODEL // 2)
    return _combine(x1, yg, gk, fg, _pick(T, (256,)))


def _pick(n, prefs):
    for t in prefs:
        if n % t == 0:
            return t
    raise ValueError(f"no tile for {n}")


def _layer(x, k_past, v_past, S0, conv0, lam_init, wts):
    (norm1_g, w_main, w_vt, w_gates, lamv, subg, conv_w, alog_p, dtb_p, gn, wo, g2, wr, br,
     wg, bg, wu, bu, wd, bd) = wts
    B, L, _ = x.shape
    T = B * L
    xt = x.reshape(T, D_MODEL)
    tm = _pick(T, (256,))
    qb, q2, k, kb, v, vb, vt, conv_in, z, gates = _in_proj(xt, norm1_g, w_main, w_vt, w_gates, tm)
    r3 = lambda a: a.reshape(B, L, a.shape[-1])
    if k_past is None:
        oa = _attn_prompt(r3(q2), r3(kb), vt, lamv, subg.reshape(DA_V_DIM, 1), lam_init,
                          _pick(L, (1024, 512, 256)), _pick(L, (512, 256)))
    else:
        P = k_past.shape[1]
        oa = _attn_sample(r3(qb), r3(kb), r3(vb), k_past.reshape(B, P, DA_QK), v_past.reshape(B, P, DA_V),
                          lamv, subg, lam_init)
    conv_in3 = r3(conv_in)
    conv0p = jnp.concatenate([jnp.zeros((B, _PAD - (GDN_CONV - 1), GDN_CONV_CH), F32), conv0.astype(F32)], axis=1)
    og, s_new = _gdn(conv_in3, r3(gates), r3(z), conv0p, S0.astype(F32), conv_w, alog_p, dtb_p, gn,
                     _pick(L // CHUNK, (4, 2, 1)))
    x1, h2, comb, cnt = _post_mix(xt, oa.reshape(T, DA_V), og.reshape(T, GDN_V), wo, g2, wr, br, tm)
    return (x1, h2, comb, cnt, k.reshape(B, L, DA_HEADS, 2, DA_HEAD_DIM), v.reshape(B, L, DA_HEADS, DA_V_DIM),
            s_new, conv_in3[:, L - (GDN_CONV - 1):, :])


def kernel(x_prompt, x_sample, cache_k, cache_v, state_ssm, state_conv, norm1_g, w_in, lambda_q1, lambda_k1,
           lambda_q2, lambda_k2, da_subln_g, conv_w, a_log, dt_bias, gdn_norm_g, w_out, norm2_g, w_router,
           b_router, w_gate, b_gate, w_up, b_up, w_down, b_down, final_g):
    depth = w_in.shape[0]
    assert depth == 1, "single-layer trunk"
    l = 0
    lam_init = 0.8 - 0.6 * math.exp(-0.3 * l)
    bp = x_prompt.shape[0]
    pad_lanes = lambda a, fill=0.0: jnp.concatenate(
        [a.astype(F32), jnp.full(a.shape[:-1] + (LANES - a.shape[-1],), fill, F32)], axis=-1)
    w_gates = pad_lanes(w_in[l][:, OFF_B:])
    zeros4 = jnp.zeros((GDN_HEADS,), F32)
    alog_p = pad_lanes(jnp.concatenate([zeros4, a_log[l].astype(F32)])[None, :])
    dtb_p = pad_lanes(jnp.concatenate([zeros4, dt_bias[l].astype(F32)])[None, :])
    wts = (
        norm1_g[l][None, :].astype(F32),
        w_in[l][:, :OFF_B].astype(BF16),
        w_in[l][:, OFF_VA:OFF_CONV].T.astype(BF16),
        w_gates,
        jnp.stack([lambda_q1[l], lambda_k1[l], lambda_q2[l], lambda_k2[l]]).astype(F32),
        da_subln_g[l][None, :].astype(F32),
        conv_w[l].astype(F32),
        alog_p, dtb_p,
        gdn_norm_g[l][None, :].astype(F32),
        w_out[l].astype(BF16),
        norm2_g[l][None, :].astype(F32),
        pad_lanes(w_router[l]),
        pad_lanes(b_router[l][None, :], NEG_INF),
        w_gate[l].astype(F32), b_gate[l][:, None, :].astype(F32),
        w_up[l].astype(F32), b_up[l][:, None, :].astype(F32),
        w_down[l].astype(F32), b_down[l][:, None, :].astype(F32),
    )
    moe_w = wts[14:]
    fg = final_g[None, :].astype(F32)

    s0 = jnp.zeros((bp, GDN_HEADS, GDN_HEAD_DIM, GDN_HEAD_DIM), x_prompt.dtype)
    c0 = jnp.zeros((bp, GDN_CONV - 1, GDN_CONV_CH), x_prompt.dtype)
    x1p, h2p, combp, cntp, k1, v1, s1, c1 = _layer(x_prompt, None, None, s0, c0, lam_init, wts)
    x1s, h2s, combs, cnts, k2, v2, s2, c2 = _layer(x_sample, cache_k[l], cache_v[l], state_ssm[l], state_conv[l],
                                                   lam_init, wts)
    y_p = _moe(h2p, combp, cntp, x1p, *moe_w, fg, 512)
    y_s = _moe(h2s, combs, cnts, x1s, *moe_w, fg, 128)
    st = lambda a: a[None]
    return (y_p.reshape(x_prompt.shape), y_s.reshape(x_sample.shape), st(k1), st(v1), st(s1), st(c1),
            st(k2), st(v2), st(s2), st(c2))
```

```python
import functools
import math

import jax
import jax.numpy as jnp
from jax import lax
from jax.experimental import pallas as pl
from jax.experimental.pallas import tpu as pltpu
from jax.experimental.pallas import tpu_sc as plsc

F32 = jnp.float32
BF16 = jnp.bfloat16

D_MODEL = 1024
CHUNK = 64
RMS_EPS = 1e-6
NEG_INF = -1e30
DA_HEADS = 4
DA_HEAD_DIM = 64
DA_V_DIM = 128
DA_QK = 512
DA_V = 512
GDN_HEADS = 4
GDN_HEAD_DIM = 128
GDN_K = 512
GDN_V = 512
GDN_CONV = 4
GDN_CONV_CH = 1536
OFF_KA = DA_QK
OFF_VA = 2 * DA_QK
OFF_CONV = 2 * DA_QK + DA_V
OFF_Z = OFF_CONV + GDN_CONV_CH
OFF_B = OFF_Z + GDN_V
N_EXPERTS = 32
TOP_K = 4
D_FF = 1024
SWIGLU_LIMIT = 7.0
SWIGLU_ALPHA = 1.702
LANES = 128
VMEM_LIMIT = 56 * 1024 * 1024
LOG2E = 1.4426950408889634


def _split_bf16(a):
    hi = a.astype(BF16)
    lo = (a - hi.astype(F32)).astype(BF16)
    return hi, lo


def _dot(a, b, dims=(((1,), (0,)), ((), ()))):
    return lax.dot_general(a, b, dims, preferred_element_type=F32)


def _dot3(a, b, dims=(((1,), (0,)), ((), ()))):
    ah, al = _split_bf16(a)
    bh, bl = _split_bf16(b)
    return _dot(ah, bh, dims) + (_dot(ah, bl, dims) + _dot(al, bh, dims))


_NT = (((1,), (1,)), ((), ()))


def _rms(x, g):
    return x * lax.rsqrt(jnp.mean(x * x, axis=-1, keepdims=True) + RMS_EPS) * g


def _pack_bf16_pairs(x):
    n = x.shape[1] // 2
    lo = lax.bitcast_convert_type(x[:, :n].astype(BF16).astype(F32), jnp.uint32)
    hi = lax.bitcast_convert_type(x[:, n:].astype(BF16).astype(F32), jnp.uint32)
    return jnp.bitwise_or(lax.shift_right_logical(lo, jnp.uint32(16)), hi)


def _unpack_bf16_pairs(w):
    lo = lax.bitcast_convert_type(lax.shift_left(w, jnp.uint32(16)), F32)
    hi = lax.bitcast_convert_type(jnp.bitwise_and(w, jnp.uint32(0xFFFF0000)), F32)
    return jnp.concatenate([lo, hi], axis=1)


_VB = 256
_VR = DA_V_DIM + 16


def _inproj_kernel(x_ref, g_ref, w_ref, wvt_ref, wg_ref,
                   qb_ref, q2_ref, k_ref, kb_ref, v_ref, vb_ref, vt_ref, cv_ref, z_ref, gt_ref):
    h = _rms(x_ref[...], g_ref[...])
    hb = h.astype(BF16)
    q = _dot(hb, w_ref[:, 0:OFF_KA])
    qb_ref[...] = (q * (DA_HEAD_DIM ** -0.5)).astype(BF16)
    q2_ref[...] = (q * (DA_HEAD_DIM ** -0.5 * LOG2E)).astype(BF16)
    k = _dot(hb, w_ref[:, OFF_KA:OFF_VA])
    k_ref[...] = k
    kb_ref[...] = k.astype(BF16)
    v = _dot(hb, w_ref[:, OFF_VA:OFF_CONV])
    v_ref[...] = v
    vb_ref[...] = v.astype(BF16)
    vt = _dot(wvt_ref[...], hb, _NT).astype(BF16)
    sub = lax.broadcasted_iota(jnp.int32, (_VR - DA_V_DIM, _VB), 0)
    ones_row = jnp.where(sub == 0, 1.0, 0.0).astype(BF16)
    for hd in range(DA_HEADS):
        for t in range(vt.shape[1] // _VB):
            vt_ref[hd, t, 0:DA_V_DIM, :] = vt[hd * DA_V_DIM:(hd + 1) * DA_V_DIM, t * _VB:(t + 1) * _VB]
            vt_ref[hd, t, DA_V_DIM:_VR, :] = ones_row
    for c in range(GDN_CONV_CH // 512):
        cv_ref[:, c * 512:(c + 1) * 512] = _dot(hb, w_ref[:, OFF_CONV + c * 512:OFF_CONV + (c + 1) * 512])
    z_ref[...] = _dot(hb, w_ref[:, OFF_Z:OFF_B])
    gt_ref[...] = _dot3(h, wg_ref[...])


def _in_proj(x, norm1_g, w_main, w_vt, w_gates, tm):
    T = x.shape[0]
    assert tm % _VB == 0
    row = lambda n: pl.BlockSpec((tm, n), lambda i: (i, 0))
    full = lambda a: pl.BlockSpec(a.shape, lambda i: (0,) * a.ndim)
    out_shape = (
        jax.ShapeDtypeStruct((T, DA_QK), BF16),
        jax.ShapeDtypeStruct((T, DA_QK), BF16),
        jax.ShapeDtypeStruct((T, DA_QK), F32),
        jax.ShapeDtypeStruct((T, DA_QK), BF16),
        jax.ShapeDtypeStruct((T, DA_V), F32),
        jax.ShapeDtypeStruct((T, DA_V), BF16),
        jax.ShapeDtypeStruct((DA_HEADS, T // _VB, _VR, _VB), BF16),
        jax.ShapeDtypeStruct((T, GDN_CONV_CH), F32),
        jax.ShapeDtypeStruct((T, GDN_V), F32),
        jax.ShapeDtypeStruct((T, LANES), F32),
    )
    out_specs = tuple(
        pl.BlockSpec((DA_HEADS, tm // _VB, _VR, _VB), lambda i: (0, i, 0, 0)) if len(s.shape) == 4
        else row(s.shape[1]) for s in out_shape)
    return pl.pallas_call(
        _inproj_kernel,
        grid=(T // tm,),
        in_specs=[row(D_MODEL), full(norm1_g), full(w_main), full(w_vt), full(w_gates)],
        out_specs=out_specs,
        out_shape=out_shape,
        compiler_params=pltpu.CompilerParams(dimension_semantics=("arbitrary",), vmem_limit_bytes=VMEM_LIMIT),
        name="in_proj",
    )(x, norm1_g, w_main, w_vt, w_gates)


def _lam_from(lamv_ref, lam_init):
    lv = lamv_ref[...]
    s1 = jnp.sum(lv[0:1, :] * lv[1:2, :], axis=-1, keepdims=True)
    s2 = jnp.sum(lv[2:3, :] * lv[3:4, :], axis=-1, keepdims=True)
    return jnp.exp(s1) - jnp.exp(s2) + lam_init


def _stack_maps(q):
    lane = lax.broadcasted_iota(jnp.int32, q.shape, 1)
    zero = jnp.zeros_like(q)
    return jnp.concatenate([jnp.where(lane < DA_HEAD_DIM, q, zero), jnp.where(lane >= DA_HEAD_DIM, q, zero)], axis=0)


def _attn_finish(acc, l, lam, subg, lam_init, tq):
    o = acc[:tq] / l[:tq] - lam * (acc[tq:] / l[tq:])
    return _rms(o, subg) * (1.0 - lam_init)


_CT = 256
_QK_SPLIT = 4


def _attn_prompt_kernel(lamv_ref, subg_ref, q_ref, k_ref, vt_ref, o_ref, qs_sc, m_sc, acc_sc, s_sc, pb_sc,
                        al_sc, *, tq, tk, lam_init):
    i = pl.program_id(2)
    nq = 2 * tq
    nct = nq // _CT
    nvt = tk // _VB
    kpq = tq // tk
    qt = jnp.transpose(q_ref[...].astype(F32))
    d = lax.broadcasted_iota(jnp.int32, qt.shape, 0)
    qs_sc[:, 0:tq] = jnp.where(d < DA_HEAD_DIM, qt, 0.0).astype(BF16)
    qs_sc[:, tq:nq] = jnp.where(d >= DA_HEAD_DIM, qt, 0.0).astype(BF16)
    m_sc[...] = jnp.full(m_sc.shape, NEG_INF, F32)
    acc_sc[...] = jnp.zeros(acc_sc.shape, F32)

    def kblock(j):
        return k_ref[pl.ds(pl.multiple_of(j * tk, tk), tk), :]

    def qk(kb, c):
        q = qs_sc[:, c * _CT:(c + 1) * _CT]
        h = tk // _QK_SPLIT
        return jnp.concatenate([_dot(kb[r * h:(r + 1) * h], q) for r in range(_QK_SPLIT)], axis=0)

    def soft(c, s, rel):
        cs = slice(c * _CT, (c + 1) * _CT)
        if rel is not None:
            key = lax.broadcasted_iota(jnp.int32, s.shape, 0) + rel
            qry = lax.broadcasted_iota(jnp.int32, s.shape, 1) + ((c * _CT) % tq)
            s = jnp.where((qry // CHUNK) >= (key // CHUNK), s, NEG_INF)
        m_prev = m_sc[:, cs]
        m_new = jnp.maximum(m_prev, jnp.max(s, axis=0, keepdims=True))
        alpha = jnp.exp2(m_prev - m_new)
        p = jnp.exp2(s - m_new)
        m_sc[:, cs] = m_new
        return p.astype(BF16), alpha

    def pv(j, c, pb, alpha):
        cs = slice(c * _CT, (c + 1) * _CT)
        upd = _dot(vt_ref[j * nvt], pb[0:_VB, :])
        for t in range(1, nvt):
            upd = upd + _dot(vt_ref[j * nvt + t], pb[t * _VB:(t + 1) * _VB, :])
        acc_sc[:, cs] = alpha * acc_sc[:, cs] + upd

    s_sc[...] = qk(kblock(0), 0)
    pb_sc[...] = jnp.zeros(pb_sc.shape, BF16)
    al_sc[...] = jnp.ones(al_sc.shape, F32)

    def run(steps, j_prev, c_prev, nxt):
        s_cur = s_sc[...]
        pend = None
        for n, (j, c, rel) in enumerate(steps):
            if n + 1 < len(steps):
                s_nxt = qk(kblock(steps[n + 1][0]), steps[n + 1][1])
            elif nxt is not None:
                s_sc[...] = qk(kblock(nxt[0]), nxt[1])
            pb, alpha = soft(c, s_cur, rel)
            if n == 0:
                pv(j_prev, c_prev, pb_sc[...], al_sc[...])
            else:
                pv(steps[n - 1][0], steps[n - 1][1], *pend)
            pend = (pb, alpha)
            if n + 1 < len(steps):
                s_cur = s_nxt
        pb_sc[...] = pend[0]
        al_sc[...] = pend[1]

    def body(j, carry):
        run([(j, c, None) for c in range(nct)], jnp.maximum(j - 1, 0), nct - 1, (j + 1, 0))
        return carry

    j0 = i * kpq
    lax.fori_loop(0, j0, body, 0)
    diag = []
    for r in range(kpq):
        for c in range(nct):
            q_lo = (c * _CT) % tq
            if q_lo + _CT <= r * tk:
                continue
            diag.append((j0 + r, c, None if q_lo >= (r + 1) * tk else r * tk))
    run(diag, jnp.maximum(j0 - 1, 0), nct - 1, None)
    pv(diag[-1][0], diag[-1][1], pb_sc[...], al_sc[...])
    lam = _lam_from(lamv_ref, lam_init)
    acc = acc_sc[0:DA_V_DIM, :]
    l = acc_sc[DA_V_DIM:DA_V_DIM + 1, :]
    ot = acc[:, 0:tq] / l[:, 0:tq] - lam * (acc[:, tq:nq] / l[:, tq:nq])
    ot = ot * lax.rsqrt(jnp.mean(ot * ot, axis=0, keepdims=True) + RMS_EPS) * subg_ref[...] * (1.0 - lam_init)
    o_ref[...] = jnp.transpose(ot).astype(o_ref.dtype)


def _attn_prompt(qb, kb, vt, lamv, subg_col, lam_init, tq, tk):
    B, L, _ = qb.shape
    assert tk % _VB == 0 and (2 * tq) % _CT == 0 and tq % tk == 0 and L % tq == 0 and _CT <= tk
    kern = functools.partial(_attn_prompt_kernel, tq=tq, tk=tk, lam_init=lam_init)
    nvb = L // _VB
    return pl.pallas_call(
        kern,
        grid=(B, DA_HEADS, L // tq),
        in_specs=[
            pl.BlockSpec(lamv.shape, lambda b, h, i: (0, 0)),
            pl.BlockSpec(subg_col.shape, lambda b, h, i: (0, 0)),
            pl.BlockSpec((None, tq, LANES), lambda b, h, i: (b, i, h)),
            pl.BlockSpec((None, L, LANES), lambda b, h, i: (b, 0, h)),
            pl.BlockSpec((None, nvb, _VR, _VB), lambda b, h, i: (h, b, 0, 0)),
        ],
        out_specs=pl.BlockSpec((None, tq, LANES), lambda b, h, i: (b, i, h)),
        out_shape=jax.ShapeDtypeStruct((B, L, DA_V), BF16),
        scratch_shapes=[
            pltpu.VMEM((LANES, 2 * tq), BF16),
            pltpu.VMEM((1, 2 * tq), F32),
            pltpu.VMEM((_VR, 2 * tq), F32),
            pltpu.VMEM((tk, _CT), F32),
            pltpu.VMEM((tk, _CT), BF16),
            pltpu.VMEM((1, _CT), F32),
        ],
        compiler_params=pltpu.CompilerParams(
            dimension_semantics=("arbitrary", "arbitrary", "arbitrary"), vmem_limit_bytes=VMEM_LIMIT),
        name="attn_prompt",
    )(lamv, subg_col, qb, kb, vt)


def _attn_sample_kernel(lamv_ref, subg_ref, q_ref, kp_ref, vp_ref, kn_ref, vn_ref, o_ref, *, lam_init):
    tq = q_ref.shape[0]
    qs = _stack_maps(q_ref[...])
    sp = _dot(qs, kp_ref[...].astype(BF16), _NT)
    sn = _dot(qs, kn_ref[...], _NT)
    m = jnp.maximum(jnp.max(sp, axis=1, keepdims=True), jnp.max(sn, axis=1, keepdims=True))
    pp = jnp.exp(sp - m)
    pn = jnp.exp(sn - m)
    l = jnp.sum(pp, axis=1, keepdims=True) + jnp.sum(pn, axis=1, keepdims=True)
    acc = _dot(pp.astype(BF16), vp_ref[...].astype(BF16)) + _dot(pn.astype(BF16), vn_ref[...])
    lam = _lam_from(lamv_ref, lam_init)
    o_ref[...] = _attn_finish(acc, l, lam, subg_ref[...], lam_init, tq).astype(o_ref.dtype)


def _attn_sample(qb, kb, vb, cache_k, cache_v, lamv, subg, lam_init):
    B, L, _ = qb.shape
    P = cache_k.shape[1]
    kern = functools.partial(_attn_sample_kernel, lam_init=lam_init)
    new = pl.BlockSpec((None, L, LANES), lambda b, h: (b, 0, h))
    past = pl.BlockSpec((None, P, LANES), lambda b, h: (b, 0, h))
    return pl.pallas_call(
        kern,
        grid=(B, DA_HEADS),
        in_specs=[
            pl.BlockSpec(lamv.shape, lambda b, h: (0, 0)),
            pl.BlockSpec(subg.shape, lambda b, h: (0, 0)),
            new, past, past, new, new,
        ],
        out_specs=new,
        out_shape=jax.ShapeDtypeStruct((B, L, DA_V), BF16),
        compiler_params=pltpu.CompilerParams(dimension_semantics=("arbitrary", "arbitrary")),
        name="attn_sample",
    )(lamv, subg, qb, cache_k, cache_v, kb, vb)


_PAD = 8


def _gdn_kernel(u_ref, gt_ref, z_ref, c0_ref, s0_ref, cw_ref, alog_ref, dtb_ref, gn_ref,
                og_ref, sout_ref, ubuf, s_sc, *, nchunks):
    c = pl.program_id(1)
    nc = pl.num_programs(1)
    C = CHUNK
    R = C * nchunks

    @pl.when(c == 0)
    def _():
        ubuf[0:_PAD, :] = c0_ref[...]
        s_sc[...] = s0_ref[...]

    u = u_ref[...]
    ubuf[_PAD:_PAD + R, :] = u
    cw = cw_ref[...]
    y = u * cw[3:4, :]
    for j in range(1, GDN_CONV):
        y = y + ubuf[_PAD - j:_PAD - j + R, :] * cw[3 - j:4 - j, :]
    ubuf[0:_PAD, :] = u[R - _PAD:, :]
    cq = y * jax.nn.sigmoid(y)

    gt = gt_ref[...]
    beta_all = jax.nn.sigmoid(gt)
    xg = gt + dtb_ref[...]
    softplus = jnp.maximum(xg, 0.0) + jnp.log1p(jnp.exp(-jnp.abs(xg)))
    g_all = -jnp.exp(alog_ref[...]) * softplus
    row = lax.broadcasted_iota(jnp.int32, (C, C), 0)
    col = lax.broadcasted_iota(jnp.int32, (C, C), 1)
    causal = row >= col
    strict = row > col
    eye = jnp.where(row == col, 1.0, 0.0).astype(F32)
    rr = lax.broadcasted_iota(jnp.int32, (R, R), 0)
    cc = lax.broadcasted_iota(jnp.int32, (R, R), 1)
    trilb = jnp.where(jnp.logical_and(rr >= cc, (rr // C) == (cc // C)), 1.0, 0.0).astype(BF16)
    ghi = g_all.astype(BF16)
    r1 = g_all - ghi.astype(F32)
    gmid = r1.astype(BF16)
    glo = (r1 - gmid.astype(F32)).astype(BF16)
    G_all = _dot(trilb, ghi) + (_dot(trilb, gmid) + _dot(trilb, glo))
    G_t = jnp.transpose(G_all)
    gn = gn_ref[...]

    qn, kn = [], []
    for h in range(GDN_HEADS):
        q = cq[:, h * GDN_HEAD_DIM:(h + 1) * GDN_HEAD_DIM]
        k = cq[:, GDN_K + h * GDN_HEAD_DIM:GDN_K + (h + 1) * GDN_HEAD_DIM]
        qn.append(q * lax.rsqrt(jnp.sum(q * q, axis=-1, keepdims=True) + 1e-6) * (GDN_HEAD_DIM ** -0.5))
        kn.append(k * lax.rsqrt(jnp.sum(k * k, axis=-1, keepdims=True) + 1e-6))

    keys = [(n, h) for n in range(nchunks) for h in range(GDN_HEADS)]
    st = {}
    for (n, h) in keys:
        rs = slice(n * C, (n + 1) * C)
        q = qn[h][rs]
        k = kn[h][rs]
        v = cq[rs, 2 * GDN_K + h * GDN_HEAD_DIM:2 * GDN_K + (h + 1) * GDN_HEAD_DIM]
        beta = beta_all[rs, h:h + 1]
        Gc = G_all[rs, GDN_HEADS + h:GDN_HEADS + h + 1]
        Gr = G_t[GDN_HEADS + h:GDN_HEADS + h + 1, rs]
        Gl = Gc[C - 1:C, :]
        decay = jnp.where(causal, jnp.exp(jnp.where(causal, Gc - Gr, 0.0)), 0.0)
        eG = jnp.exp(Gc)
        kb = k * beta
        st[n, h] = dict(decay=decay, kbf=k.astype(BF16), kkb=kb.astype(BF16), qb=q.astype(BF16),
                        rhs=jnp.concatenate([v * beta, kb * eG], axis=1),
                        qg=(q * eG).astype(BF16), kg=k * jnp.exp(Gl - Gc), gl=jnp.exp(Gl))
    for key in keys:
        d = st[key]
        d["kk"] = _dot(d["kkb"], d["kbf"], _NT)
    for key in keys:
        d = st[key]
        d["qk"] = (_dot(d["qb"], d["kbf"], _NT) * d["decay"]).astype(BF16)
    for key in keys:
        d = st[key]
        nm = jnp.where(strict, d["kk"] * d["decay"], 0.0)
        d["inv"] = eye - nm
        d["pw"] = nm.astype(BF16)
    for _ in range(5):
        for key in keys:
            d = st[key]
            d["pwf"] = _dot(d["pw"], d["pw"])
        for key in keys:
            d = st[key]
            d["pw"] = d["pwf"].astype(BF16)
            d["inv"] = d["inv"] + _dot(d["inv"].astype(BF16), d["pw"])
    for key in keys:
        d = st[key]
        invb = d["inv"].astype(BF16)
        rh, rl = _split_bf16(d["rhs"])
        sol = _dot(invb, rh) + _dot(invb, rl)
        d["u"] = sol[:, :GDN_HEAD_DIM]
        d["w"] = sol[:, GDN_HEAD_DIM:].astype(BF16)
        d["kgt"] = jnp.transpose(d["kg"]).astype(BF16)

    S = [s_sc[h] for h in range(GDN_HEADS)]
    heads = range(GDN_HEADS)
    for n in range(nchunks):
        rs = slice(n * C, (n + 1) * C)
        Sb = [S[h].astype(BF16) for h in heads]
        ws = [_dot(st[n, h]["w"], Sb[h]) for h in heads]
        qs = [_dot(st[n, h]["qg"], Sb[h]) for h in heads]
        vb = [(st[n, h]["u"] - ws[h]).astype(BF16) for h in heads]
        o = [qs[h] + _dot(st[n, h]["qk"], vb[h]) for h in heads]
        for h in heads:
            S[h] = S[h] * st[n, h]["gl"] + _dot(st[n, h]["kgt"], vb[h])
        for h in heads:
            sl = slice(h * GDN_HEAD_DIM, (h + 1) * GDN_HEAD_DIM)
            zz = z_ref[rs, sl]
            og_ref[rs, sl] = (_rms(o[h], gn) * (zz * jax.nn.sigmoid(zz))).astype(og_ref.dtype)
    for h in heads:
        s_sc[h] = S[h]

    @pl.when(c == nc - 1)
    def _():
        sout_ref[...] = s_sc[...]


def _gdn(conv_in, gates, z, conv0p, s0, conv_w, alog_p, dtb_p, gn, nchunks):
    B, L, _ = conv_in.shape
    R = CHUNK * nchunks
    assert L % R == 0
    tok = lambda n: pl.BlockSpec((None, R, n), lambda b, c: (b, c, 0))
    full = lambda a: pl.BlockSpec(a.shape, lambda b, c: (0,) * a.ndim)
    return pl.pallas_call(
        functools.partial(_gdn_kernel, nchunks=nchunks),
        grid=(B, L // R),
        in_specs=[
            tok(GDN_CONV_CH), tok(LANES), tok(GDN_V),
            pl.BlockSpec((None, _PAD, GDN_CONV_CH), lambda b, c: (b, 0, 0)),
            pl.BlockSpec((None, GDN_HEADS, GDN_HEAD_DIM, GDN_HEAD_DIM), lambda b, c: (b, 0, 0, 0)),
            full(conv_w), full(alog_p), full(dtb_p), full(gn),
        ],
        out_specs=(
            tok(GDN_V),
            pl.BlockSpec((None, GDN_HEADS, GDN_HEAD_DIM, GDN_HEAD_DIM), lambda b, c: (b, 0, 0, 0)),
        ),
        out_shape=(
            jax.ShapeDtypeStruct((B, L, GDN_V), BF16),
            jax.ShapeDtypeStruct((B, GDN_HEADS, GDN_HEAD_DIM, GDN_HEAD_DIM), F32),
        ),
        scratch_shapes=[
            pltpu.VMEM((_PAD + R, GDN_CONV_CH), F32),
            pltpu.VMEM((GDN_HEADS, GDN_HEAD_DIM, GDN_HEAD_DIM), F32),
        ],
        compiler_params=pltpu.CompilerParams(dimension_semantics=("arbitrary", "arbitrary"),
                                             vmem_limit_bytes=VMEM_LIMIT),
        name="gdn",
    )(conv_in, gates, z, conv0p, s0, conv_w, alog_p, dtb_p, gn)


def _postmix_kernel(x_ref, oa_ref, og_ref, wo_ref, g2_ref, wr_ref, br_ref, x1_ref, h2_ref, comb_ref, cnt_ref, cnt_sc):
    i = pl.program_id(0)

    @pl.when(i == 0)
    def _():
        cnt_sc[...] = jnp.zeros(cnt_sc.shape, F32)

    mix = _dot(oa_ref[...], wo_ref[0:DA_V, :]) + _dot(og_ref[...], wo_ref[DA_V:, :])
    x1 = x_ref[...] + mix
    x1_ref[...] = x1
    h2 = _rms(x1, g2_ref[...])
    h2_ref[...] = _pack_bf16_pairs(h2)
    logits = _dot3(h2, wr_ref[...]) + br_ref[...]
    lane = lax.broadcasted_iota(jnp.int32, logits.shape, 1).astype(F32)
    work = logits
    sel = jnp.zeros(logits.shape, jnp.bool_)
    top = jnp.max(work, axis=1, keepdims=True)
    for _ in range(TOP_K):
        mx = jnp.max(work, axis=1, keepdims=True)
        first = jnp.min(jnp.where(work == mx, lane, float(LANES)), axis=1, keepdims=True)
        pick = lane == first
        sel = jnp.logical_or(sel, pick)
        work = jnp.where(pick, -jnp.inf, work)
    e = jnp.where(sel, jnp.exp(logits - top), 0.0)
    flags = pltpu.roll(jnp.where(sel, 1.0, 0.0), N_EXPERTS, axis=1)
    comb_ref[...] = e / jnp.sum(e, axis=1, keepdims=True) + flags
    cnt_sc[...] = cnt_sc[...] + jnp.sum(flags, axis=0, keepdims=True)
    cnt_ref[...] = cnt_sc[...]


def _post_mix(x, oa, og, wo, g2, wr, br, tm):
    T = x.shape[0]
    row = lambda n: pl.BlockSpec((tm, n), lambda i: (i, 0))
    full = lambda a: pl.BlockSpec(a.shape, lambda i: (0,) * a.ndim)
    return pl.pallas_call(
        _postmix_kernel,
        grid=(T // tm,),
        in_specs=[row(D_MODEL), row(DA_V), row(GDN_V), full(wo), full(g2), full(wr), full(br)],
        out_specs=(row(D_MODEL), row(D_MODEL // 2), row(LANES), pl.BlockSpec((1, LANES), lambda i: (0, 0))),
        out_shape=(
            jax.ShapeDtypeStruct((T, D_MODEL), F32),
            jax.ShapeDtypeStruct((T, D_MODEL // 2), jnp.uint32),
            jax.ShapeDtypeStruct((T, LANES), F32),
            jax.ShapeDtypeStruct((1, LANES), F32),
        ),
        scratch_shapes=[pltpu.VMEM((1, LANES), F32)],
        compiler_params=pltpu.CompilerParams(dimension_semantics=("arbitrary",), vmem_limit_bytes=VMEM_LIMIT),
        name="post_mix",
    )(x, oa, og, wo, g2, wr, br)


_RT = 512


def _route_kernel(comb_ref, off_ref, slot_ref, gk_ref, carry_sc):
    i = pl.program_id(0)

    @pl.when(i == 0)
    def _():
        carry_sc[...] = jnp.zeros(carry_sc.shape, F32)

    comb = comb_ref[...]
    tr = comb.shape[0]
    lane = lax.broadcasted_iota(jnp.int32, comb.shape, 1).astype(F32)
    sel = jnp.where(lane >= float(N_EXPERTS), comb, 0.0)
    r = lax.broadcasted_iota(jnp.int32, (tr, tr), 0)
    c = lax.broadcasted_iota(jnp.int32, (tr, tr), 1)
    below = jnp.where(r > c, 1.0, 0.0).astype(BF16)
    pos = _dot(below, sel.astype(BF16)) + (carry_sc[...] + off_ref[...])
    carry_sc[...] = carry_sc[...] + jnp.sum(sel, axis=0, keepdims=True)
    rest = sel
    slot = jnp.zeros(comb.shape, F32)
    gk = jnp.zeros(comb.shape, F32)
    for k in range(TOP_K):
        first = jnp.min(jnp.where(rest > 0.0, lane, float(LANES)), axis=1, keepdims=True)
        hit = lane == first
        expert = first - float(N_EXPERTS)
        slot = jnp.where(lane == float(k), jnp.sum(jnp.where(hit, pos, 0.0), axis=1, keepdims=True), slot)
        gk = jnp.where(lane == float(k), jnp.sum(jnp.where(lane == expert, comb, 0.0), axis=1, keepdims=True), gk)
        rest = jnp.where(hit, 0.0, rest)
    slot_ref[...] = slot.astype(jnp.int32)
    gk_ref[...] = gk


def _route(comb, off_lanes):
    T = comb.shape[0]
    tr = _pick(T, (_RT, 256, 128, 64))
    row = pl.BlockSpec((tr, LANES), lambda i: (i, 0))
    return pl.pallas_call(
        _route_kernel,
        grid=(T // tr,),
        in_specs=[row, pl.BlockSpec((1, LANES), lambda i: (0, 0))],
        out_specs=(row, row),
        out_shape=(
            jax.ShapeDtypeStruct((T, LANES), jnp.int32),
            jax.ShapeDtypeStruct((T, LANES), F32),
        ),
        scratch_shapes=[pltpu.VMEM((1, LANES), F32)],
        compiler_params=pltpu.CompilerParams(dimension_semantics=("arbitrary",)),
        name="route",
    )(comb, off_lanes)


_SC_WIN = 128
_SC_COLS = 128


def _sc_mesh():
    return plsc.VectorSubcoreMesh(core_axis_name="c", subcore_axis_name="s")


def _scatter_rows(x, idx, n_out):
    K, T = idx.shape
    D = x.shape[1]
    assert T % _SC_WIN == 0 and D % _SC_COLS == 0
    flat = idx.reshape(1, K * T)
    nt = T // _SC_WIN

    @pl.kernel(out_type=jax.ShapeDtypeStruct((n_out, D), x.dtype), mesh=_sc_mesh(), scratch_types=[])
    def kern(x_hbm, i_hbm, o_hbm):
        for dc in range(D // _SC_COLS):
            def body(x_vmem, i_vmem, dc=dc):
                pltpu.sync_copy(x_vmem, o_hbm.at[i_vmem.at[0], pl.ds(dc * _SC_COLS, _SC_COLS)])

            pltpu.emit_pipeline(
                body,
                grid=(K * nt,),
                in_specs=[pl.BlockSpec((_SC_WIN, _SC_COLS), lambda i, dc=dc: (i % nt, dc)),
                          pl.BlockSpec((1, _SC_WIN), lambda i: (0, i))],
                out_specs=[],
                core_axis_name=("c", "s"),
                dimension_semantics=(pltpu.PARALLEL,),
            )(x_hbm, i_hbm)

    return kern(x, flat)


def _gather_rows(y, idx):
    K, T = idx.shape
    D = y.shape[1]
    assert (K * T) % _SC_WIN == 0 and D % _SC_COLS == 0
    flat = idx.reshape(1, K * T)

    @pl.kernel(out_type=jax.ShapeDtypeStruct((K * T, D), y.dtype), mesh=_sc_mesh(), scratch_types=[])
    def kern(y_hbm, i_hbm, o_hbm):
        for dc in range(D // _SC_COLS):
            def body(i_vmem, o_vmem, dc=dc):
                pltpu.sync_copy(y_hbm.at[i_vmem.at[0], pl.ds(dc * _SC_COLS, _SC_COLS)], o_vmem)

            pltpu.emit_pipeline(
                body,
                grid=(K * T // _SC_WIN,),
                in_specs=[pl.BlockSpec((1, _SC_WIN), lambda i: (0, i))],
                out_specs=[pl.BlockSpec((_SC_WIN, _SC_COLS), lambda i, dc=dc: (i, dc))],
                core_axis_name=("c", "s"),
                dimension_semantics=(pltpu.PARALLEL,),
            )(i_hbm, o_hbm)

    return kern(y, flat)


_FF_BLK = 512


def _expert_kernel(te_ref, na_ref, x_ref, wg_ref, bg_ref, wu_ref, bu_ref, wd_ref, bd_ref, y_ref, *wb_refs, cast):
    n = pl.program_id(0)
    if cast:
        wgb, wub, wdb = wb_refs

        @pl.when(jnp.logical_or(n == 0, te_ref[n] != te_ref[jnp.maximum(n - 1, 0)]))
        def _():
            wgb[...] = wg_ref[...].astype(BF16)
            wub[...] = wu_ref[...].astype(BF16)
            wdb[...] = wd_ref[...].astype(BF16)
    else:
        wgb, wub, wdb = wg_ref, wu_ref, wd_ref

    @pl.when(n < na_ref[0])
    def _():
        h = _unpack_bf16_pairs(x_ref[...]).astype(BF16)
        o = bd_ref[...]
        for f in range(D_FF // _FF_BLK):
            fs = slice(f * _FF_BLK, (f + 1) * _FF_BLK)
            gt = jnp.minimum(_dot(h, wgb[:, fs]) + bg_ref[:, fs], SWIGLU_LIMIT)
            up = jnp.clip(_dot(h, wub[:, fs]) + bu_ref[:, fs], -SWIGLU_LIMIT, SWIGLU_LIMIT)
            act = (up + 1.0) * gt * jax.nn.sigmoid(SWIGLU_ALPHA * gt)
            o = o + _dot(act.astype(BF16), wdb[fs, :])
        y_ref[...] = _pack_bf16_pairs(o)


def _experts(xs, tile_expert, n_active, wg, bg, wu, bu, wd, bd, tmr):
    S = xs.shape[0]
    cast = wg.dtype != BF16
    rows = pl.BlockSpec((tmr, D_MODEL // 2), lambda n, te, na: (jnp.minimum(n, na[0] - 1), 0))
    wspec = pl.BlockSpec((None, D_MODEL, D_FF), lambda n, te, na: (te[n], 0, 0))
    bspec = pl.BlockSpec((None, 1, D_FF), lambda n, te, na: (te[n], 0, 0))
    y_shape = jax.ShapeDtypeStruct((S, D_MODEL // 2), jnp.uint32)
    wb_shape = jax.ShapeDtypeStruct((N_EXPERTS, D_MODEL, D_FF), BF16)
    out = pl.pallas_call(
        functools.partial(_expert_kernel, cast=cast),
        grid_spec=pltpu.PrefetchScalarGridSpec(
            num_scalar_prefetch=2,
            grid=(S // tmr,),
            in_specs=[rows, wspec, bspec, wspec, bspec, wspec, bspec],
            out_specs=(rows, wspec, wspec, wspec) if cast else rows,
        ),
        out_shape=(y_shape, wb_shape, wb_shape, wb_shape) if cast else y_shape,
        compiler_params=pltpu.CompilerParams(dimension_semantics=("arbitrary",), vmem_limit_bytes=VMEM_LIMIT),
        name="experts",
    )(tile_expert, n_active, xs, wg, bg, wu, bu, wd, bd)
    return (out[0], out[1:]) if cast else (out, (wg, wu, wd))


def _combine_kernel(x1_ref, yg_ref, gk_ref, fg_ref, y_ref):
    gk = gk_ref[...]
    moe = gk[:, 0:1] * _unpack_bf16_pairs(yg_ref[0])
    for k in range(1, TOP_K):
        moe = moe + gk[:, k:k + 1] * _unpack_bf16_pairs(yg_ref[k])
    y_ref[...] = _rms(x1_ref[...] + moe, fg_ref[...])


def _combine(x1, yg, gk, fg, tm):
    T = x1.shape[0]
    row = lambda n: pl.BlockSpec((tm, n), lambda i: (i, 0))
    return pl.pallas_call(
        _combine_kernel,
        grid=(T // tm,),
        in_specs=[row(D_MODEL), pl.BlockSpec((TOP_K, tm, D_MODEL // 2), lambda i: (0, i, 0)), row(LANES),
                  pl.BlockSpec(fg.shape, lambda i: (0, 0))],
        out_specs=row(D_MODEL),
        out_shape=jax.ShapeDtypeStruct((T, D_MODEL), F32),
        compiler_params=pltpu.CompilerParams(dimension_semantics=("arbitrary",), vmem_limit_bytes=VMEM_LIMIT),
        name="combine",
    )(x1, yg, gk, fg)


def _moe(h2, comb, cnt, x1, wg, bg, wu, bu, wd, bd, fg, tmr):
    T = h2.shape[0]
    cnt_e = cnt[0, N_EXPERTS:2 * N_EXPERTS].astype(jnp.int32)
    tiles_e = jnp.maximum((cnt_e + tmr - 1) // tmr, 1)
    tile_end = jnp.cumsum(tiles_e)
    off = (tile_end - tiles_e) * tmr
    n_tiles = T * TOP_K // tmr + N_EXPERTS
    n_active = tile_end[-1:].astype(jnp.int32)
    tile_id = jnp.minimum(jnp.arange(n_tiles, dtype=jnp.int32), n_active[0] - 1)
    tile_expert = jnp.sum((tile_end[None, :] <= tile_id[:, None]).astype(jnp.int32), axis=1)
    tile_expert = jnp.minimum(tile_expert, N_EXPERTS - 1).astype(jnp.int32)
    off_lanes = jnp.zeros((1, LANES), F32).at[0, N_EXPERTS:2 * N_EXPERTS].set(off.astype(F32))
    slot128, gk = _route(comb, off_lanes)
    slot = slot128[:, :TOP_K].T
    xs = _scatter_rows(h2, slot, n_tiles * tmr)
    ys, wb = _experts(xs, tile_expert, n_active, wg, bg, wu, bu, wd, bd, tmr)
    yg = _gather_rows(ys, slot).reshape(TOP_K, T, D_MODEL // 2)
    return _combine(x1, yg, gk, fg, _pick(T, (256,))), wb


def _pick(n, prefs):
    for t in prefs:
        if n % t == 0:
            return t
    raise ValueError(f"no tile for {n}")


def _layer(x, k_past, v_past, S0, conv0, lam_init, wts):
    (norm1_g, w_main, w_vt, w_gates, lamv, subg, conv_w, alog_p, dtb_p, gn, wo, g2, wr, br,
     wg, bg, wu, bu, wd, bd) = wts
    B, L, _ = x.shape
    T = B * L
    xt = x.reshape(T, D_MODEL)
    tm = _pick(T, (256,))
    qb, q2, k, kb, v, vb, vt, conv_in, z, gates = _in_proj(xt, norm1_g, w_main, w_vt, w_gates, tm)
    r3 = lambda a: a.reshape(B, L, a.shape[-1])
    if k_past is None:
        oa = _attn_prompt(r3(q2), r3(kb), vt, lamv, subg.reshape(DA_V_DIM, 1), lam_init,
                          _pick(L, (1024, 512, 256)), _pick(L, (512, 256)))
    else:
        P = k_past.shape[1]
        oa = _attn_sample(r3(qb), r3(kb), r3(vb), k_past.reshape(B, P, DA_QK), v_past.reshape(B, P, DA_V),
                          lamv, subg, lam_init)
    conv_in3 = r3(conv_in)
    conv0p = jnp.concatenate([jnp.zeros((B, _PAD - (GDN_CONV - 1), GDN_CONV_CH), F32), conv0.astype(F32)], axis=1)
    og, s_new = _gdn(conv_in3, r3(gates), r3(z), conv0p, S0.astype(F32), conv_w, alog_p, dtb_p, gn,
                     _pick(L // CHUNK, (4, 2, 1)))
    x1, h2, comb, cnt = _post_mix(xt, oa.reshape(T, DA_V), og.reshape(T, GDN_V), wo, g2, wr, br, tm)
    return (x1, h2, comb, cnt, k.reshape(B, L, DA_HEADS, 2, DA_HEAD_DIM), v.reshape(B, L, DA_HEADS, DA_V_DIM),
            s_new, conv_in3[:, L - (GDN_CONV - 1):, :])


def kernel(x_prompt, x_sample, cache_k, cache_v, state_ssm, state_conv, norm1_g, w_in, lambda_q1, lambda_k1,
           lambda_q2, lambda_k2, da_subln_g, conv_w, a_log, dt_bias, gdn_norm_g, w_out, norm2_g, w_router,
           b_router, w_gate, b_gate, w_up, b_up, w_down, b_down, final_g):
    depth = w_in.shape[0]
    assert depth == 1, "single-layer trunk"
    l = 0
    lam_init = 0.8 - 0.6 * math.exp(-0.3 * l)
    bp = x_prompt.shape[0]
    pad_lanes = lambda a, fill=0.0: jnp.concatenate(
        [a.astype(F32), jnp.full(a.shape[:-1] + (LANES - a.shape[-1],), fill, F32)], axis=-1)
    w_gates = pad_lanes(w_in[l][:, OFF_B:])
    zeros4 = jnp.zeros((GDN_HEADS,), F32)
    alog_p = pad_lanes(jnp.concatenate([zeros4, a_log[l].astype(F32)])[None, :])
    dtb_p = pad_lanes(jnp.concatenate([zeros4, dt_bias[l].astype(F32)])[None, :])
    wts = (
        norm1_g[l][None, :].astype(F32),
        w_in[l][:, :OFF_B].astype(BF16),
        w_in[l][:, OFF_VA:OFF_CONV].T.astype(BF16),
        w_gates,
        jnp.stack([lambda_q1[l], lambda_k1[l], lambda_q2[l], lambda_k2[l]]).astype(F32),
        da_subln_g[l][None, :].astype(F32),
        conv_w[l].astype(F32),
        alog_p, dtb_p,
        gdn_norm_g[l][None, :].astype(F32),
        w_out[l].astype(BF16),
        norm2_g[l][None, :].astype(F32),
        pad_lanes(w_router[l]),
        pad_lanes(b_router[l][None, :], NEG_INF),
        w_gate[l].astype(F32), b_gate[l][:, None, :].astype(F32),
        w_up[l].astype(F32), b_up[l][:, None, :].astype(F32),
        w_down[l].astype(F32), b_down[l][:, None, :].astype(F32),
    )
    moe_w = wts[14:]
    fg = final_g[None, :].astype(F32)

    s0 = jnp.zeros((bp, GDN_HEADS, GDN_HEAD_DIM, GDN_HEAD_DIM), x_prompt.dtype)
    c0 = jnp.zeros((bp, GDN_CONV - 1, GDN_CONV_CH), x_prompt.dtype)
    x1p, h2p, combp, cntp, k1, v1, s1, c1 = _layer(x_prompt, None, None, s0, c0, lam_init, wts)
    x1s, h2s, combs, cnts, k2, v2, s2, c2 = _layer(x_sample, cache_k[l], cache_v[l], state_ssm[l], state_conv[l],
                                                   lam_init, wts)
    wg, bg, wu, bu, wd, bd = moe_w
    y_p, (wgb, wub, wdb) = _moe(h2p, combp, cntp, x1p, wg, bg, wu, bu, wd, bd, fg, 512)
    y_s, _ = _moe(h2s, combs, cnts, x1s, wgb, bg, wub, bu, wdb, bd, fg, 128)
    st = lambda a: a[None]
    return (y_p.reshape(x_prompt.shape), y_s.reshape(x_sample.shape), st(k1), st(v1), st(s1), st(c1),
            st(k2), st(v2), st(s2), st(c2))
```

```python
import functools
import math

import jax
import jax.numpy as jnp
from jax import lax
from jax.experimental import pallas as pl
from jax.experimental.pallas import tpu as pltpu
from jax.experimental.pallas import tpu_sc as plsc

F32 = jnp.float32
BF16 = jnp.bfloat16

D_MODEL = 1024
CHUNK = 64
RMS_EPS = 1e-6
NEG_INF = -1e30
DA_HEADS = 4
DA_HEAD_DIM = 64
DA_V_DIM = 128
DA_QK = 512
DA_V = 512
GDN_HEADS = 4
GDN_HEAD_DIM = 128
GDN_K = 512
GDN_V = 512
GDN_CONV = 4
GDN_CONV_CH = 1536
OFF_KA = DA_QK
OFF_VA = 2 * DA_QK
OFF_CONV = 2 * DA_QK + DA_V
OFF_Z = OFF_CONV + GDN_CONV_CH
OFF_B = OFF_Z + GDN_V
N_EXPERTS = 32
TOP_K = 4
D_FF = 1024
SWIGLU_LIMIT = 7.0
SWIGLU_ALPHA = 1.702
LANES = 128
VMEM_LIMIT = 56 * 1024 * 1024
LOG2E = 1.4426950408889634


def _split_bf16(a):
    hi = a.astype(BF16)
    lo = (a - hi.astype(F32)).astype(BF16)
    return hi, lo


def _dot(a, b, dims=(((1,), (0,)), ((), ()))):
    return lax.dot_general(a, b, dims, preferred_element_type=F32)


def _dot3(a, b, dims=(((1,), (0,)), ((), ()))):
    ah, al = _split_bf16(a)
    bh, bl = _split_bf16(b)
    return _dot(ah, bh, dims) + (_dot(ah, bl, dims) + _dot(al, bh, dims))


_NT = (((1,), (1,)), ((), ()))


def _rms(x, g):
    return x * lax.rsqrt(jnp.mean(x * x, axis=-1, keepdims=True) + RMS_EPS) * g


def _pack_bf16_pairs(x):
    n = x.shape[1] // 2
    lo = lax.bitcast_convert_type(x[:, :n].astype(BF16).astype(F32), jnp.uint32)
    hi = lax.bitcast_convert_type(x[:, n:].astype(BF16).astype(F32), jnp.uint32)
    return jnp.bitwise_or(lax.shift_right_logical(lo, jnp.uint32(16)), hi)


def _unpack_bf16_pairs(w):
    lo = lax.bitcast_convert_type(lax.shift_left(w, jnp.uint32(16)), F32)
    hi = lax.bitcast_convert_type(jnp.bitwise_and(w, jnp.uint32(0xFFFF0000)), F32)
    return jnp.concatenate([lo, hi], axis=1)


_VB = 256
_VR = DA_V_DIM + 16


def _inproj_kernel(x_ref, g_ref, w_ref, wvt_ref, wg_ref,
                   qb_ref, q2_ref, k_ref, kb_ref, v_ref, vb_ref, vt_ref, cv_ref, z_ref, gt_ref):
    h = _rms(x_ref[...], g_ref[...])
    hb = h.astype(BF16)
    q = _dot(hb, w_ref[:, 0:OFF_KA])
    qb_ref[...] = (q * (DA_HEAD_DIM ** -0.5)).astype(BF16)
    q2_ref[...] = (q * (DA_HEAD_DIM ** -0.5 * LOG2E)).astype(BF16)
    k = _dot(hb, w_ref[:, OFF_KA:OFF_VA])
    k_ref[...] = k
    kb_ref[...] = k.astype(BF16)
    v = _dot(hb, w_ref[:, OFF_VA:OFF_CONV])
    v_ref[...] = v
    vb_ref[...] = v.astype(BF16)
    vt = _dot(wvt_ref[...], hb, _NT).astype(BF16)
    sub = lax.broadcasted_iota(jnp.int32, (_VR - DA_V_DIM, _VB), 0)
    ones_row = jnp.where(sub == 0, 1.0, 0.0).astype(BF16)
    for hd in range(DA_HEADS):
        for t in range(vt.shape[1] // _VB):
            vt_ref[hd, t, 0:DA_V_DIM, :] = vt[hd * DA_V_DIM:(hd + 1) * DA_V_DIM, t * _VB:(t + 1) * _VB]
            vt_ref[hd, t, DA_V_DIM:_VR, :] = ones_row
    for c in range(GDN_CONV_CH // 512):
        cv_ref[:, c * 512:(c + 1) * 512] = _dot(hb, w_ref[:, OFF_CONV + c * 512:OFF_CONV + (c + 1) * 512])
    z_ref[...] = _dot(hb, w_ref[:, OFF_Z:OFF_B])
    gt_ref[...] = _dot3(h, wg_ref[...])


def _in_proj(x, norm1_g, w_main, w_vt, w_gates, tm):
    T = x.shape[0]
    assert tm % _VB == 0
    row = lambda n: pl.BlockSpec((tm, n), lambda i: (i, 0))
    full = lambda a: pl.BlockSpec(a.shape, lambda i: (0,) * a.ndim)
    out_shape = (
        jax.ShapeDtypeStruct((T, DA_QK), BF16),
        jax.ShapeDtypeStruct((T, DA_QK), BF16),
        jax.ShapeDtypeStruct((T, DA_QK), F32),
        jax.ShapeDtypeStruct((T, DA_QK), BF16),
        jax.ShapeDtypeStruct((T, DA_V), F32),
        jax.ShapeDtypeStruct((T, DA_V), BF16),
        jax.ShapeDtypeStruct((DA_HEADS, T // _VB, _VR, _VB), BF16),
        jax.ShapeDtypeStruct((T, GDN_CONV_CH), F32),
        jax.ShapeDtypeStruct((T, GDN_V), F32),
        jax.ShapeDtypeStruct((T, LANES), F32),
    )
    out_specs = tuple(
        pl.BlockSpec((DA_HEADS, tm // _VB, _VR, _VB), lambda i: (0, i, 0, 0)) if len(s.shape) == 4
        else row(s.shape[1]) for s in out_shape)
    return pl.pallas_call(
        _inproj_kernel,
        grid=(T // tm,),
        in_specs=[row(D_MODEL), full(norm1_g), full(w_main), full(w_vt), full(w_gates)],
        out_specs=out_specs,
        out_shape=out_shape,
        compiler_params=pltpu.CompilerParams(dimension_semantics=("arbitrary",), vmem_limit_bytes=VMEM_LIMIT),
        name="in_proj",
    )(x, norm1_g, w_main, w_vt, w_gates)


def _lam_from(lamv_ref, lam_init):
    lv = lamv_ref[...]
    s1 = jnp.sum(lv[0:1, :] * lv[1:2, :], axis=-1, keepdims=True)
    s2 = jnp.sum(lv[2:3, :] * lv[3:4, :], axis=-1, keepdims=True)
    return jnp.exp(s1) - jnp.exp(s2) + lam_init


def _stack_maps(q):
    lane = lax.broadcasted_iota(jnp.int32, q.shape, 1)
    zero = jnp.zeros_like(q)
    return jnp.concatenate([jnp.where(lane < DA_HEAD_DIM, q, zero), jnp.where(lane >= DA_HEAD_DIM, q, zero)], axis=0)


def _attn_finish(acc, l, lam, subg, lam_init, tq):
    o = acc[:tq] / l[:tq] - lam * (acc[tq:] / l[tq:])
    return _rms(o, subg) * (1.0 - lam_init)


_CT = 256
_QK_SPLIT = 4


def _attn_prompt_kernel(lamv_ref, subg_ref, q_ref, k_ref, vt_ref, o_ref, qs_sc, m_sc, acc_sc, s_sc, pb_sc,
                        al_sc, *, tq, tk, lam_init):
    i = pl.program_id(2)
    nq = 2 * tq
    nct = nq // _CT
    nvt = tk // _VB
    kpq = tq // tk
    qt = jnp.transpose(q_ref[...].astype(F32))
    d = lax.broadcasted_iota(jnp.int32, qt.shape, 0)
    qs_sc[:, 0:tq] = jnp.where(d < DA_HEAD_DIM, qt, 0.0).astype(BF16)
    qs_sc[:, tq:nq] = jnp.where(d >= DA_HEAD_DIM, qt, 0.0).astype(BF16)
    m_sc[...] = jnp.full(m_sc.shape, NEG_INF, F32)
    acc_sc[...] = jnp.zeros(acc_sc.shape, F32)

    def kblock(j):
        return k_ref[pl.ds(pl.multiple_of(j * tk, tk), tk), :]

    def qk(kb, c):
        q = qs_sc[:, c * _CT:(c + 1) * _CT]
        h = tk // _QK_SPLIT
        return jnp.concatenate([_dot(kb[r * h:(r + 1) * h], q) for r in range(_QK_SPLIT)], axis=0)

    def soft(c, s, rel):
        cs = slice(c * _CT, (c + 1) * _CT)
        if rel is not None:
            key = lax.broadcasted_iota(jnp.int32, s.shape, 0) + rel
            qry = lax.broadcasted_iota(jnp.int32, s.shape, 1) + ((c * _CT) % tq)
            s = jnp.where((qry // CHUNK) >= (key // CHUNK), s, NEG_INF)
        m_prev = m_sc[:, cs]
        m_new = jnp.maximum(m_prev, jnp.max(s, axis=0, keepdims=True))
        alpha = jnp.exp2(m_prev - m_new)
        p = jnp.exp2(s - m_new)
        m_sc[:, cs] = m_new
        return p.astype(BF16), alpha

    def pv(j, c, pb, alpha):
        cs = slice(c * _CT, (c + 1) * _CT)
        upd = _dot(vt_ref[j * nvt], pb[0:_VB, :])
        for t in range(1, nvt):
            upd = upd + _dot(vt_ref[j * nvt + t], pb[t * _VB:(t + 1) * _VB, :])
        acc_sc[:, cs] = alpha * acc_sc[:, cs] + upd

    s_sc[...] = qk(kblock(0), 0)
    pb_sc[...] = jnp.zeros(pb_sc.shape, BF16)
    al_sc[...] = jnp.ones(al_sc.shape, F32)

    def run(steps, j_prev, c_prev, nxt):
        s_cur = s_sc[...]
        pend = None
        for n, (j, c, rel) in enumerate(steps):
            if n + 1 < len(steps):
                s_nxt = qk(kblock(steps[n + 1][0]), steps[n + 1][1])
            elif nxt is not None:
                s_sc[...] = qk(kblock(nxt[0]), nxt[1])
            pb, alpha = soft(c, s_cur, rel)
            if n == 0:
                pv(j_prev, c_prev, pb_sc[...], al_sc[...])
            else:
                pv(steps[n - 1][0], steps[n - 1][1], *pend)
            pend = (pb, alpha)
            if n + 1 < len(steps):
                s_cur = s_nxt
        pb_sc[...] = pend[0]
        al_sc[...] = pend[1]

    def body(j, carry):
        run([(j, c, None) for c in range(nct)], jnp.maximum(j - 1, 0), nct - 1, (j + 1, 0))
        return carry

    j0 = i * kpq
    lax.fori_loop(0, j0, body, 0)
    diag = []
    for r in range(kpq):
        for c in range(nct):
            q_lo = (c * _CT) % tq
            if q_lo + _CT <= r * tk:
                continue
            diag.append((j0 + r, c, None if q_lo >= (r + 1) * tk else r * tk))
    run(diag, jnp.maximum(j0 - 1, 0), nct - 1, None)
    pv(diag[-1][0], diag[-1][1], pb_sc[...], al_sc[...])
    lam = _lam_from(lamv_ref, lam_init)
    acc = acc_sc[0:DA_V_DIM, :]
    l = acc_sc[DA_V_DIM:DA_V_DIM + 1, :]
    ot = acc[:, 0:tq] / l[:, 0:tq] - lam * (acc[:, tq:nq] / l[:, tq:nq])
    ot = ot * lax.rsqrt(jnp.mean(ot * ot, axis=0, keepdims=True) + RMS_EPS) * subg_ref[...] * (1.0 - lam_init)
    o_ref[...] = jnp.transpose(ot).astype(o_ref.dtype)


def _attn_prompt(qb, kb, vt, lamv, subg_col, lam_init, tq, tk):
    B, L, _ = qb.shape
    assert tk % _VB == 0 and (2 * tq) % _CT == 0 and tq % tk == 0 and L % tq == 0 and _CT <= tk
    kern = functools.partial(_attn_prompt_kernel, tq=tq, tk=tk, lam_init=lam_init)
    nvb = L // _VB
    return pl.pallas_call(
        kern,
        grid=(B, DA_HEADS, L // tq),
        in_specs=[
            pl.BlockSpec(lamv.shape, lambda b, h, i: (0, 0)),
            pl.BlockSpec(subg_col.shape, lambda b, h, i: (0, 0)),
            pl.BlockSpec((None, tq, LANES), lambda b, h, i: (b, i, h)),
            pl.BlockSpec((None, L, LANES), lambda b, h, i: (b, 0, h)),
            pl.BlockSpec((None, nvb, _VR, _VB), lambda b, h, i: (h, b, 0, 0)),
        ],
        out_specs=pl.BlockSpec((None, tq, LANES), lambda b, h, i: (b, i, h)),
        out_shape=jax.ShapeDtypeStruct((B, L, DA_V), BF16),
        scratch_shapes=[
            pltpu.VMEM((LANES, 2 * tq), BF16),
            pltpu.VMEM((1, 2 * tq), F32),
            pltpu.VMEM((_VR, 2 * tq), F32),
            pltpu.VMEM((tk, _CT), F32),
            pltpu.VMEM((tk, _CT), BF16),
            pltpu.VMEM((1, _CT), F32),
        ],
        compiler_params=pltpu.CompilerParams(
            dimension_semantics=("arbitrary", "arbitrary", "arbitrary"), vmem_limit_bytes=VMEM_LIMIT),
        name="attn_prompt",
    )(lamv, subg_col, qb, kb, vt)


def _attn_sample_kernel(lamv_ref, subg_ref, q_ref, kp_ref, vp_ref, kn_ref, vn_ref, o_ref, *, lam_init):
    tq = q_ref.shape[0]
    qs = _stack_maps(q_ref[...])
    sp = _dot(qs, kp_ref[...].astype(BF16), _NT)
    sn = _dot(qs, kn_ref[...], _NT)
    m = jnp.maximum(jnp.max(sp, axis=1, keepdims=True), jnp.max(sn, axis=1, keepdims=True))
    pp = jnp.exp(sp - m)
    pn = jnp.exp(sn - m)
    l = jnp.sum(pp, axis=1, keepdims=True) + jnp.sum(pn, axis=1, keepdims=True)
    acc = _dot(pp.astype(BF16), vp_ref[...].astype(BF16)) + _dot(pn.astype(BF16), vn_ref[...])
    lam = _lam_from(lamv_ref, lam_init)
    o_ref[...] = _attn_finish(acc, l, lam, subg_ref[...], lam_init, tq).astype(o_ref.dtype)


def _attn_sample(qb, kb, vb, cache_k, cache_v, lamv, subg, lam_init):
    B, L, _ = qb.shape
    P = cache_k.shape[1]
    kern = functools.partial(_attn_sample_kernel, lam_init=lam_init)
    new = pl.BlockSpec((None, L, LANES), lambda b, h: (b, 0, h))
    past = pl.BlockSpec((None, P, LANES), lambda b, h: (b, 0, h))
    return pl.pallas_call(
        kern,
        grid=(B, DA_HEADS),
        in_specs=[
            pl.BlockSpec(lamv.shape, lambda b, h: (0, 0)),
            pl.BlockSpec(subg.shape, lambda b, h: (0, 0)),
            new, past, past, new, new,
        ],
        out_specs=new,
        out_shape=jax.ShapeDtypeStruct((B, L, DA_V), BF16),
        compiler_params=pltpu.CompilerParams(dimension_semantics=("arbitrary", "arbitrary")),
        name="attn_sample",
    )(lamv, subg, qb, cache_k, cache_v, kb, vb)


_PAD = 8


def _gdn_kernel(u_ref, gt_ref, z_ref, c0_ref, s0_ref, cw_ref, alog_ref, dtb_ref, gn_ref,
                og_ref, sout_ref, ubuf, s_sc, *, nchunks):
    c = pl.program_id(1)
    nc = pl.num_programs(1)
    C = CHUNK
    R = C * nchunks

    @pl.when(c == 0)
    def _():
        ubuf[0:_PAD, :] = c0_ref[...]
        s_sc[...] = s0_ref[...]

    u = u_ref[...]
    ubuf[_PAD:_PAD + R, :] = u
    cw = cw_ref[...]
    y = u * cw[3:4, :]
    for j in range(1, GDN_CONV):
        y = y + ubuf[_PAD - j:_PAD - j + R, :] * cw[3 - j:4 - j, :]
    ubuf[0:_PAD, :] = u[R - _PAD:, :]
    cq = y * jax.nn.sigmoid(y)

    gt = gt_ref[...]
    beta_all = jax.nn.sigmoid(gt)
    xg = gt + dtb_ref[...]
    softplus = jnp.maximum(xg, 0.0) + jnp.log1p(jnp.exp(-jnp.abs(xg)))
    g_all = -jnp.exp(alog_ref[...]) * softplus
    row = lax.broadcasted_iota(jnp.int32, (C, C), 0)
    col = lax.broadcasted_iota(jnp.int32, (C, C), 1)
    causal = row >= col
    strict = row > col
    eye = jnp.where(row == col, 1.0, 0.0).astype(F32)
    rr = lax.broadcasted_iota(jnp.int32, (R, R), 0)
    cc = lax.broadcasted_iota(jnp.int32, (R, R), 1)
    trilb = jnp.where(jnp.logical_and(rr >= cc, (rr // C) == (cc // C)), 1.0, 0.0).astype(BF16)
    ghi = g_all.astype(BF16)
    r1 = g_all - ghi.astype(F32)
    gmid = r1.astype(BF16)
    glo = (r1 - gmid.astype(F32)).astype(BF16)
    G_all = _dot(trilb, ghi) + (_dot(trilb, gmid) + _dot(trilb, glo))
    G_t = jnp.transpose(G_all)
    gn = gn_ref[...]

    qn, kn = [], []
    for h in range(GDN_HEADS):
        q = cq[:, h * GDN_HEAD_DIM:(h + 1) * GDN_HEAD_DIM]
        k = cq[:, GDN_K + h * GDN_HEAD_DIM:GDN_K + (h + 1) * GDN_HEAD_DIM]
        qn.append(q * lax.rsqrt(jnp.sum(q * q, axis=-1, keepdims=True) + 1e-6) * (GDN_HEAD_DIM ** -0.5))
        kn.append(k * lax.rsqrt(jnp.sum(k * k, axis=-1, keepdims=True) + 1e-6))

    keys = [(n, h) for n in range(nchunks) for h in range(GDN_HEADS)]
    st = {}
    for (n, h) in keys:
        rs = slice(n * C, (n + 1) * C)
        q = qn[h][rs]
        k = kn[h][rs]
        v = cq[rs, 2 * GDN_K + h * GDN_HEAD_DIM:2 * GDN_K + (h + 1) * GDN_HEAD_DIM]
        beta = beta_all[rs, h:h + 1]
        Gc = G_all[rs, GDN_HEADS + h:GDN_HEADS + h + 1]
        Gr = G_t[GDN_HEADS + h:GDN_HEADS + h + 1, rs]
        Gl = Gc[C - 1:C, :]
        decay = jnp.where(causal, jnp.exp(jnp.where(causal, Gc - Gr, 0.0)), 0.0)
        eG = jnp.exp(Gc)
        kb = k * beta
        st[n, h] = dict(decay=decay, kbf=k.astype(BF16), kkb=kb.astype(BF16), qb=q.astype(BF16),
                        rhs=jnp.concatenate([v * beta, kb * eG], axis=1),
                        qg=(q * eG).astype(BF16), kg=k * jnp.exp(Gl - Gc), gl=jnp.exp(Gl))
    for key in keys:
        d = st[key]
        d["kk"] = _dot(d["kkb"], d["kbf"], _NT)
    for key in keys:
        d = st[key]
        d["qk"] = (_dot(d["qb"], d["kbf"], _NT) * d["decay"]).astype(BF16)
    for key in keys:
        d = st[key]
        nm = jnp.where(strict, d["kk"] * d["decay"], 0.0)
        d["inv"] = eye - nm
        d["pw"] = nm.astype(BF16)
    for _ in range(5):
        for key in keys:
            d = st[key]
            d["pwf"] = _dot(d["pw"], d["pw"])
        for key in keys:
            d = st[key]
            d["pw"] = d["pwf"].astype(BF16)
            d["inv"] = d["inv"] + _dot(d["inv"].astype(BF16), d["pw"])
    for key in keys:
        d = st[key]
        invb = d["inv"].astype(BF16)
        rh, rl = _split_bf16(d["rhs"])
        sol = _dot(invb, rh) + _dot(invb, rl)
        d["u"] = sol[:, :GDN_HEAD_DIM]
        d["w"] = sol[:, GDN_HEAD_DIM:].astype(BF16)
        d["kgt"] = jnp.transpose(d["kg"]).astype(BF16)

    S = [s_sc[h] for h in range(GDN_HEADS)]
    heads = range(GDN_HEADS)
    for n in range(nchunks):
        rs = slice(n * C, (n + 1) * C)
        Sb = [S[h].astype(BF16) for h in heads]
        ws = [_dot(st[n, h]["w"], Sb[h]) for h in heads]
        qs = [_dot(st[n, h]["qg"], Sb[h]) for h in heads]
        vb = [(st[n, h]["u"] - ws[h]).astype(BF16) for h in heads]
        o = [qs[h] + _dot(st[n, h]["qk"], vb[h]) for h in heads]
        for h in heads:
            S[h] = S[h] * st[n, h]["gl"] + _dot(st[n, h]["kgt"], vb[h])
        for h in heads:
            sl = slice(h * GDN_HEAD_DIM, (h + 1) * GDN_HEAD_DIM)
            zz = z_ref[rs, sl]
            og_ref[rs, sl] = (_rms(o[h], gn) * (zz * jax.nn.sigmoid(zz))).astype(og_ref.dtype)
    for h in heads:
        s_sc[h] = S[h]

    @pl.when(c == nc - 1)
    def _():
        sout_ref[...] = s_sc[...]


def _gdn(conv_in, gates, z, conv0p, s0, conv_w, alog_p, dtb_p, gn, nchunks):
    B, L, _ = conv_in.shape
    R = CHUNK * nchunks
    assert L % R == 0
    tok = lambda n: pl.BlockSpec((None, R, n), lambda b, c: (b, c, 0))
    full = lambda a: pl.BlockSpec(a.shape, lambda b, c: (0,) * a.ndim)
    return pl.pallas_call(
        functools.partial(_gdn_kernel, nchunks=nchunks),
        grid=(B, L // R),
        in_specs=[
            tok(GDN_CONV_CH), tok(LANES), tok(GDN_V),
            pl.BlockSpec((None, _PAD, GDN_CONV_CH), lambda b, c: (b, 0, 0)),
            pl.BlockSpec((None, GDN_HEADS, GDN_HEAD_DIM, GDN_HEAD_DIM), lambda b, c: (b, 0, 0, 0)),
            full(conv_w), full(alog_p), full(dtb_p), full(gn),
        ],
        out_specs=(
            tok(GDN_V),
            pl.BlockSpec((None, GDN_HEADS, GDN_HEAD_DIM, GDN_HEAD_DIM), lambda b, c: (b, 0, 0, 0)),
        ),
        out_shape=(
            jax.ShapeDtypeStruct((B, L, GDN_V), BF16),
            jax.ShapeDtypeStruct((B, GDN_HEADS, GDN_HEAD_DIM, GDN_HEAD_DIM), F32),
        ),
        scratch_shapes=[
            pltpu.VMEM((_PAD + R, GDN_CONV_CH), F32),
            pltpu.VMEM((GDN_HEADS, GDN_HEAD_DIM, GDN_HEAD_DIM), F32),
        ],
        compiler_params=pltpu.CompilerParams(dimension_semantics=("arbitrary", "arbitrary"),
                                             vmem_limit_bytes=VMEM_LIMIT),
        name="gdn",
    )(conv_in, gates, z, conv0p, s0, conv_w, alog_p, dtb_p, gn)


def _postmix_kernel(x_ref, oa_ref, og_ref, wo_ref, g2_ref, wr_ref, br_ref, x1_ref, h2_ref, comb_ref, cnt_ref, cnt_sc):
    i = pl.program_id(0)

    @pl.when(i == 0)
    def _():
        cnt_sc[...] = jnp.zeros(cnt_sc.shape, F32)

    mix = _dot(oa_ref[...], wo_ref[0:DA_V, :]) + _dot(og_ref[...], wo_ref[DA_V:, :])
    x1 = x_ref[...] + mix
    x1_ref[...] = x1
    h2 = _rms(x1, g2_ref[...])
    h2_ref[...] = _pack_bf16_pairs(h2)
    logits = _dot3(h2, wr_ref[...]) + br_ref[...]
    lane = lax.broadcasted_iota(jnp.int32, logits.shape, 1).astype(F32)
    work = logits
    sel = jnp.zeros(logits.shape, jnp.bool_)
    top = jnp.max(work, axis=1, keepdims=True)
    for _ in range(TOP_K):
        mx = jnp.max(work, axis=1, keepdims=True)
        first = jnp.min(jnp.where(work == mx, lane, float(LANES)), axis=1, keepdims=True)
        pick = lane == first
        sel = jnp.logical_or(sel, pick)
        work = jnp.where(pick, -jnp.inf, work)
    e = jnp.where(sel, jnp.exp(logits - top), 0.0)
    flags = pltpu.roll(jnp.where(sel, 1.0, 0.0), N_EXPERTS, axis=1)
    comb_ref[...] = e / jnp.sum(e, axis=1, keepdims=True) + flags
    cnt_sc[...] = cnt_sc[...] + jnp.sum(flags, axis=0, keepdims=True)
    cnt_ref[...] = cnt_sc[...]


def _post_mix(x, oa, og, wo, g2, wr, br, tm):
    T = x.shape[0]
    row = lambda n: pl.BlockSpec((tm, n), lambda i: (i, 0))
    full = lambda a: pl.BlockSpec(a.shape, lambda i: (0,) * a.ndim)
    return pl.pallas_call(
        _postmix_kernel,
        grid=(T // tm,),
        in_specs=[row(D_MODEL), row(DA_V), row(GDN_V), full(wo), full(g2), full(wr), full(br)],
        out_specs=(row(D_MODEL), row(D_MODEL // 2), row(LANES), pl.BlockSpec((1, LANES), lambda i: (0, 0))),
        out_shape=(
            jax.ShapeDtypeStruct((T, D_MODEL), F32),
            jax.ShapeDtypeStruct((T, D_MODEL // 2), jnp.uint32),
            jax.ShapeDtypeStruct((T, LANES), F32),
            jax.ShapeDtypeStruct((1, LANES), F32),
        ),
        scratch_shapes=[pltpu.VMEM((1, LANES), F32)],
        compiler_params=pltpu.CompilerParams(dimension_semantics=("arbitrary",), vmem_limit_bytes=VMEM_LIMIT),
        name="post_mix",
    )(x, oa, og, wo, g2, wr, br)


_RT = 512


def _route_kernel(comb_ref, off_ref, slot_ref, gk_ref, carry_sc):
    i = pl.program_id(0)

    @pl.when(i == 0)
    def _():
        carry_sc[...] = jnp.zeros(carry_sc.shape, F32)

    comb = comb_ref[...]
    tr = comb.shape[0]
    lane = lax.broadcasted_iota(jnp.int32, comb.shape, 1).astype(F32)
    sel = jnp.where(lane >= float(N_EXPERTS), comb, 0.0)
    r = lax.broadcasted_iota(jnp.int32, (tr, tr), 0)
    c = lax.broadcasted_iota(jnp.int32, (tr, tr), 1)
    below = jnp.where(r > c, 1.0, 0.0).astype(BF16)
    pos = _dot(below, sel.astype(BF16)) + (carry_sc[...] + off_ref[...])
    carry_sc[...] = carry_sc[...] + jnp.sum(sel, axis=0, keepdims=True)
    rest = sel
    slot = jnp.zeros(comb.shape, F32)
    gk = jnp.zeros(comb.shape, F32)
    for k in range(TOP_K):
        first = jnp.min(jnp.where(rest > 0.0, lane, float(LANES)), axis=1, keepdims=True)
        hit = lane == first
        expert = first - float(N_EXPERTS)
        slot = jnp.where(lane == float(k), jnp.sum(jnp.where(hit, pos, 0.0), axis=1, keepdims=True), slot)
        gk = jnp.where(lane == float(k), jnp.sum(jnp.where(lane == expert, comb, 0.0), axis=1, keepdims=True), gk)
        rest = jnp.where(hit, 0.0, rest)
    slot_ref[...] = slot.astype(jnp.int32)
    gk_ref[...] = gk


def _route(comb, off_lanes):
    T = comb.shape[0]
    tr = _pick(T, (_RT, 256, 128, 64))
    row = pl.BlockSpec((tr, LANES), lambda i: (i, 0))
    return pl.pallas_call(
        _route_kernel,
        grid=(T // tr,),
        in_specs=[row, pl.BlockSpec((1, LANES), lambda i: (0, 0))],
        out_specs=(row, row),
        out_shape=(
            jax.ShapeDtypeStruct((T, LANES), jnp.int32),
            jax.ShapeDtypeStruct((T, LANES), F32),
        ),
        scratch_shapes=[pltpu.VMEM((1, LANES), F32)],
        compiler_params=pltpu.CompilerParams(dimension_semantics=("arbitrary",)),
        name="route",
    )(comb, off_lanes)


_SC_WIN = 128
_SC_COLS = 128


def _sc_mesh():
    return plsc.VectorSubcoreMesh(core_axis_name="c", subcore_axis_name="s")


def _scatter_rows(x, idx, n_out):
    K, T = idx.shape
    D = x.shape[1]
    assert T % _SC_WIN == 0 and D % _SC_COLS == 0
    flat = idx.reshape(1, K * T)
    nt = T // _SC_WIN

    @pl.kernel(out_type=jax.ShapeDtypeStruct((n_out, D), x.dtype), mesh=_sc_mesh(), scratch_types=[])
    def kern(x_hbm, i_hbm, o_hbm):
        for dc in range(D // _SC_COLS):
            def body(x_vmem, i_vmem, dc=dc):
                pltpu.sync_copy(x_vmem, o_hbm.at[i_vmem.at[0], pl.ds(dc * _SC_COLS, _SC_COLS)])

            pltpu.emit_pipeline(
                body,
                grid=(K * nt,),
                in_specs=[pl.BlockSpec((_SC_WIN, _SC_COLS), lambda i, dc=dc: (i % nt, dc)),
                          pl.BlockSpec((1, _SC_WIN), lambda i: (0, i))],
                out_specs=[],
                core_axis_name=("c", "s"),
                dimension_semantics=(pltpu.PARALLEL,),
            )(x_hbm, i_hbm)

    return kern(x, flat)


def _gather_rows(y, idx):
    K, T = idx.shape
    D = y.shape[1]
    assert (K * T) % _SC_WIN == 0 and D % _SC_COLS == 0
    flat = idx.reshape(1, K * T)

    @pl.kernel(out_type=jax.ShapeDtypeStruct((K * T, D), y.dtype), mesh=_sc_mesh(), scratch_types=[])
    def kern(y_hbm, i_hbm, o_hbm):
        for dc in range(D // _SC_COLS):
            def body(i_vmem, o_vmem, dc=dc):
                pltpu.sync_copy(y_hbm.at[i_vmem.at[0], pl.ds(dc * _SC_COLS, _SC_COLS)], o_vmem)

            pltpu.emit_pipeline(
                body,
                grid=(K * T // _SC_WIN,),
                in_specs=[pl.BlockSpec((1, _SC_WIN), lambda i: (0, i))],
                out_specs=[pl.BlockSpec((_SC_WIN, _SC_COLS), lambda i, dc=dc: (i, dc))],
                core_axis_name=("c", "s"),
                dimension_semantics=(pltpu.PARALLEL,),
            )(i_hbm, o_hbm)

    return kern(y, flat)


_FF_BLK = 512


def _expert_kernel(te_ref, na_ref, x_ref, wg_ref, bg_ref, wu_ref, bu_ref, wd_ref, bd_ref, y_ref, *wb_refs, cast):
    n = pl.program_id(0)
    if cast:
        wgb, wub, wdb = wb_refs

        @pl.when(jnp.logical_or(n == 0, te_ref[n] != te_ref[jnp.maximum(n - 1, 0)]))
        def _():
            wgb[...] = wg_ref[...].astype(BF16)
            wub[...] = wu_ref[...].astype(BF16)
            wdb[...] = wd_ref[...].astype(BF16)
    else:
        wgb, wub, wdb = wg_ref, wu_ref, wd_ref

    @pl.when(n < na_ref[0])
    def _():
        h = _unpack_bf16_pairs(x_ref[...]).astype(BF16)
        o = bd_ref[...]
        for f in range(D_FF // _FF_BLK):
            fs = slice(f * _FF_BLK, (f + 1) * _FF_BLK)
            gt = jnp.minimum(_dot(h, wgb[:, fs]) + bg_ref[:, fs], SWIGLU_LIMIT)
            up = jnp.clip(_dot(h, wub[:, fs]) + bu_ref[:, fs], -SWIGLU_LIMIT, SWIGLU_LIMIT)
            act = (up + 1.0) * gt * jax.nn.sigmoid(SWIGLU_ALPHA * gt)
            o = o + _dot(act.astype(BF16), wdb[fs, :])
        y_ref[...] = _pack_bf16_pairs(o)


def _experts(xs, tile_expert, n_active, wg, bg, wu, bu, wd, bd, tmr):
    S = xs.shape[0]
    cast = wg.dtype != BF16
    rows = pl.BlockSpec((tmr, D_MODEL // 2), lambda n, te, na: (jnp.minimum(n, na[0] - 1), 0))
    wspec = pl.BlockSpec((None, D_MODEL, D_FF), lambda n, te, na: (te[n], 0, 0))
    bspec = pl.BlockSpec((None, 1, D_FF), lambda n, te, na: (te[n], 0, 0))
    y_shape = jax.ShapeDtypeStruct((S, D_MODEL // 2), jnp.uint32)
    wb_shape = jax.ShapeDtypeStruct((N_EXPERTS, D_MODEL, D_FF), BF16)
    out = pl.pallas_call(
        functools.partial(_expert_kernel, cast=cast),
        grid_spec=pltpu.PrefetchScalarGridSpec(
            num_scalar_prefetch=2,
            grid=(S // tmr,),
            in_specs=[rows, wspec, bspec, wspec, bspec, wspec, bspec],
            out_specs=(rows, wspec, wspec, wspec) if cast else rows,
        ),
        out_shape=(y_shape, wb_shape, wb_shape, wb_shape) if cast else y_shape,
        compiler_params=pltpu.CompilerParams(dimension_semantics=("arbitrary",), vmem_limit_bytes=VMEM_LIMIT),
        name="experts",
    )(tile_expert, n_active, xs, wg, bg, wu, bu, wd, bd)
    return (out[0], out[1:]) if cast else (out, (wg, wu, wd))


def _combine_kernel(x1_ref, yg_ref, gk_ref, fg_ref, y_ref):
    gk = gk_ref[...]
    moe = gk[:, 0:1] * _unpack_bf16_pairs(yg_ref[0])
    for k in range(1, TOP_K):
        moe = moe + gk[:, k:k + 1] * _unpack_bf16_pairs(yg_ref[k])
    y_ref[...] = _rms(x1_ref[...] + moe, fg_ref[...])


def _combine(x1, yg, gk, fg, tm):
    T = x1.shape[0]
    row = lambda n: pl.BlockSpec((tm, n), lambda i: (i, 0))
    return pl.pallas_call(
        _combine_kernel,
        grid=(T // tm,),
        in_specs=[row(D_MODEL), pl.BlockSpec((TOP_K, tm, D_MODEL // 2), lambda i: (0, i, 0)), row(LANES),
                  pl.BlockSpec(fg.shape, lambda i: (0, 0))],
        out_specs=row(D_MODEL),
        out_shape=jax.ShapeDtypeStruct((T, D_MODEL), F32),
        compiler_params=pltpu.CompilerParams(dimension_semantics=("arbitrary",), vmem_limit_bytes=VMEM_LIMIT),
        name="combine",
    )(x1, yg, gk, fg)


def _moe(h2, comb, cnt, x1, wg, bg, wu, bu, wd, bd, fg, tmr):
    T = h2.shape[0]
    cnt_e = cnt[0, N_EXPERTS:2 * N_EXPERTS].astype(jnp.int32)
    tiles_e = jnp.maximum((cnt_e + tmr - 1) // tmr, 1)
    tile_end = jnp.cumsum(tiles_e)
    off = (tile_end - tiles_e) * tmr
    n_tiles = T * TOP_K // tmr + N_EXPERTS
    n_active = tile_end[-1:].astype(jnp.int32)
    tile_id = jnp.minimum(jnp.arange(n_tiles, dtype=jnp.int32), n_active[0] - 1)
    tile_expert = jnp.sum((tile_end[None, :] <= tile_id[:, None]).astype(jnp.int32), axis=1)
    tile_expert = jnp.minimum(tile_expert, N_EXPERTS - 1).astype(jnp.int32)
    off_lanes = jnp.zeros((1, LANES), F32).at[0, N_EXPERTS:2 * N_EXPERTS].set(off.astype(F32))
    slot128, gk = _route(comb, off_lanes)
    slot = slot128[:, :TOP_K].T
    xs = _scatter_rows(h2, slot, n_tiles * tmr)
    ys, wb = _experts(xs, tile_expert, n_active, wg, bg, wu, bu, wd, bd, tmr)
    yg = _gather_rows(ys, slot).reshape(TOP_K, T, D_MODEL // 2)
    return _combine(x1, yg, gk, fg, _pick(T, (256,))), wb


def _pick(n, prefs):
    for t in prefs:
        if n % t == 0:
            return t
    raise ValueError(f"no tile for {n}")


def _layer(x, k_past, v_past, S0, conv0, lam_init, wts):
    (norm1_g, w_main, w_vt, w_gates, lamv, subg, conv_w, alog_p, dtb_p, gn, wo, g2, wr, br,
     wg, bg, wu, bu, wd, bd) = wts
    B, L, _ = x.shape
    T = B * L
    xt = x.reshape(T, D_MODEL)
    tm = _pick(T, (512, 256))
    qb, q2, k, kb, v, vb, vt, conv_in, z, gates = _in_proj(xt, norm1_g, w_main, w_vt, w_gates, tm)
    r3 = lambda a: a.reshape(B, L, a.shape[-1])
    if k_past is None:
        oa = _attn_prompt(r3(q2), r3(kb), vt, lamv, subg.reshape(DA_V_DIM, 1), lam_init,
                          _pick(L, (2048, 1024, 512, 256)), _pick(L, (512, 256)))
    else:
        P = k_past.shape[1]
        oa = _attn_sample(r3(qb), r3(kb), r3(vb), k_past.reshape(B, P, DA_QK), v_past.reshape(B, P, DA_V),
                          lamv, subg, lam_init)
    conv_in3 = r3(conv_in)
    conv0p = jnp.concatenate([jnp.zeros((B, _PAD - (GDN_CONV - 1), GDN_CONV_CH), F32), conv0.astype(F32)], axis=1)
    og, s_new = _gdn(conv_in3, r3(gates), r3(z), conv0p, S0.astype(F32), conv_w, alog_p, dtb_p, gn,
                     _pick(L // CHUNK, (4, 2, 1)))
    x1, h2, comb, cnt = _post_mix(xt, oa.reshape(T, DA_V), og.reshape(T, GDN_V), wo, g2, wr, br, tm)
    return (x1, h2, comb, cnt, k.reshape(B, L, DA_HEADS, 2, DA_HEAD_DIM), v.reshape(B, L, DA_HEADS, DA_V_DIM),
            s_new, conv_in3[:, L - (GDN_CONV - 1):, :])


def kernel(x_prompt, x_sample, cache_k, cache_v, state_ssm, state_conv, norm1_g, w_in, lambda_q1, lambda_k1,
           lambda_q2, lambda_k2, da_subln_g, conv_w, a_log, dt_bias, gdn_norm_g, w_out, norm2_g, w_router,
           b_router, w_gate, b_gate, w_up, b_up, w_down, b_down, final_g):
    depth = w_in.shape[0]
    assert depth == 1, "single-layer trunk"
    l = 0
    lam_init = 0.8 - 0.6 * math.exp(-0.3 * l)
    bp = x_prompt.shape[0]
    pad_lanes = lambda a, fill=0.0: jnp.concatenate(
        [a.astype(F32), jnp.full(a.shape[:-1] + (LANES - a.shape[-1],), fill, F32)], axis=-1)
    w_gates = pad_lanes(w_in[l][:, OFF_B:])
    zeros4 = jnp.zeros((GDN_HEADS,), F32)
    alog_p = pad_lanes(jnp.concatenate([zeros4, a_log[l].astype(F32)])[None, :])
    dtb_p = pad_lanes(jnp.concatenate([zeros4, dt_bias[l].astype(F32)])[None, :])
    wts = (
        norm1_g[l][None, :].astype(F32),
        w_in[l][:, :OFF_B].astype(BF16),
        w_in[l][:, OFF_VA:OFF_CONV].T.astype(BF16),
        w_gates,
        jnp.stack([lambda_q1[l], lambda_k1[l], lambda_q2[l], lambda_k2[l]]).astype(F32),
        da_subln_g[l][None, :].astype(F32),
        conv_w[l].astype(F32),
        alog_p, dtb_p,
        gdn_norm_g[l][None, :].astype(F32),
        w_out[l].astype(BF16),
        norm2_g[l][None, :].astype(F32),
        pad_lanes(w_router[l]),
        pad_lanes(b_router[l][None, :], NEG_INF),
        w_gate[l].astype(F32), b_gate[l][:, None, :].astype(F32),
        w_up[l].astype(F32), b_up[l][:, None, :].astype(F32),
        w_down[l].astype(F32), b_down[l][:, None, :].astype(F32),
    )
    moe_w = wts[14:]
    fg = final_g[None, :].astype(F32)

    s0 = jnp.zeros((bp, GDN_HEADS, GDN_HEAD_DIM, GDN_HEAD_DIM), x_prompt.dtype)
    c0 = jnp.zeros((bp, GDN_CONV - 1, GDN_CONV_CH), x_prompt.dtype)
    x1p, h2p, combp, cntp, k1, v1, s1, c1 = _layer(x_prompt, None, None, s0, c0, lam_init, wts)
    x1s, h2s, combs, cnts, k2, v2, s2, c2 = _layer(x_sample, cache_k[l], cache_v[l], state_ssm[l], state_conv[l],
                                                   lam_init, wts)
    wg, bg, wu, bu, wd, bd = moe_w
    y_p, (wgb, wub, wdb) = _moe(h2p, combp, cntp, x1p, wg, bg, wu, bu, wd, bd, fg, 512)
    y_s, _ = _moe(h2s, combs, cnts, x1s, wgb, bg, wub, bu, wdb, bd, fg, 128)
    st = lambda a: a[None]
    return (y_p.reshape(x_prompt.shape), y_s.reshape(x_sample.shape), st(k1), st(v1), st(s1), st(c1),
            st(k2), st(v2), st(s2), st(c2))
```

```python
import functools
import math

import jax
import jax.numpy as jnp
from jax import lax
from jax.experimental import pallas as pl
from jax.experimental.pallas import tpu as pltpu
from jax.experimental.pallas import tpu_sc as plsc

F32 = jnp.float32
BF16 = jnp.bfloat16

D_MODEL = 1024
CHUNK = 64
RMS_EPS = 1e-6
NEG_INF = -1e30
DA_HEADS = 4
DA_HEAD_DIM = 64
DA_V_DIM = 128
DA_QK = 512
DA_V = 512
GDN_HEADS = 4
GDN_HEAD_DIM = 128
GDN_K = 512
GDN_V = 512
GDN_CONV = 4
GDN_CONV_CH = 1536
OFF_KA = DA_QK
OFF_VA = 2 * DA_QK
OFF_CONV = 2 * DA_QK + DA_V
OFF_Z = OFF_CONV + GDN_CONV_CH
OFF_B = OFF_Z + GDN_V
N_EXPERTS = 32
TOP_K = 4
D_FF = 1024
SWIGLU_LIMIT = 7.0
SWIGLU_ALPHA = 1.702
LANES = 128
VMEM_LIMIT = 56 * 1024 * 1024
LOG2E = 1.4426950408889634


def _split_bf16(a):
    hi = a.astype(BF16)
    lo = (a - hi.astype(F32)).astype(BF16)
    return hi, lo


def _dot(a, b, dims=(((1,), (0,)), ((), ()))):
    return lax.dot_general(a, b, dims, preferred_element_type=F32)


def _dot3(a, b, dims=(((1,), (0,)), ((), ()))):
    ah, al = _split_bf16(a)
    bh, bl = _split_bf16(b)
    return _dot(ah, bh, dims) + (_dot(ah, bl, dims) + _dot(al, bh, dims))


_NT = (((1,), (1,)), ((), ()))


def _rms(x, g):
    return x * lax.rsqrt(jnp.mean(x * x, axis=-1, keepdims=True) + RMS_EPS) * g


def _pack_bf16_pairs(x):
    n = x.shape[1] // 2
    lo = lax.bitcast_convert_type(x[:, :n].astype(BF16).astype(F32), jnp.uint32)
    hi = lax.bitcast_convert_type(x[:, n:].astype(BF16).astype(F32), jnp.uint32)
    return jnp.bitwise_or(lax.shift_right_logical(lo, jnp.uint32(16)), hi)


def _unpack_bf16_pairs(w):
    lo = lax.bitcast_convert_type(lax.shift_left(w, jnp.uint32(16)), F32)
    hi = lax.bitcast_convert_type(jnp.bitwise_and(w, jnp.uint32(0xFFFF0000)), F32)
    return jnp.concatenate([lo, hi], axis=1)


_VB = 256
_VR = DA_V_DIM + 16


def _inproj_kernel(x_ref, g_ref, w_ref, wvt_ref, wg_ref,
                   qb_ref, q2_ref, k_ref, kb_ref, v_ref, vb_ref, vt_ref, cv_ref, z_ref, gt_ref):
    h = _rms(x_ref[...], g_ref[...])
    hb = h.astype(BF16)
    q = _dot(hb, w_ref[:, 0:OFF_KA])
    qb_ref[...] = (q * (DA_HEAD_DIM ** -0.5)).astype(BF16)
    q2_ref[...] = (q * (DA_HEAD_DIM ** -0.5 * LOG2E)).astype(BF16)
    k = _dot(hb, w_ref[:, OFF_KA:OFF_VA])
    k_ref[...] = k
    kb_ref[...] = k.astype(BF16)
    v = _dot(hb, w_ref[:, OFF_VA:OFF_CONV])
    v_ref[...] = v
    vb_ref[...] = v.astype(BF16)
    vt = _dot(wvt_ref[...], hb, _NT).astype(BF16)
    sub = lax.broadcasted_iota(jnp.int32, (_VR - DA_V_DIM, _VB), 0)
    ones_row = jnp.where(sub == 0, 1.0, 0.0).astype(BF16)
    for hd in range(DA_HEADS):
        for t in range(vt.shape[1] // _VB):
            vt_ref[hd, t, 0:DA_V_DIM, :] = vt[hd * DA_V_DIM:(hd + 1) * DA_V_DIM, t * _VB:(t + 1) * _VB]
            vt_ref[hd, t, DA_V_DIM:_VR, :] = ones_row
    for c in range(GDN_CONV_CH // 512):
        cv_ref[:, c * 512:(c + 1) * 512] = _dot(hb, w_ref[:, OFF_CONV + c * 512:OFF_CONV + (c + 1) * 512])
    z_ref[...] = _dot(hb, w_ref[:, OFF_Z:OFF_B])
    gt_ref[...] = _dot3(h, wg_ref[...])


def _in_proj(x, norm1_g, w_main, w_vt, w_gates, tm):
    T = x.shape[0]
    assert tm % _VB == 0
    row = lambda n: pl.BlockSpec((tm, n), lambda i: (i, 0))
    full = lambda a: pl.BlockSpec(a.shape, lambda i: (0,) * a.ndim)
    out_shape = (
        jax.ShapeDtypeStruct((T, DA_QK), BF16),
        jax.ShapeDtypeStruct((T, DA_QK), BF16),
        jax.ShapeDtypeStruct((T, DA_QK), F32),
        jax.ShapeDtypeStruct((T, DA_QK), BF16),
        jax.ShapeDtypeStruct((T, DA_V), F32),
        jax.ShapeDtypeStruct((T, DA_V), BF16),
        jax.ShapeDtypeStruct((DA_HEADS, T // _VB, _VR, _VB), BF16),
        jax.ShapeDtypeStruct((T, GDN_CONV_CH), F32),
        jax.ShapeDtypeStruct((T, GDN_V), F32),
        jax.ShapeDtypeStruct((T, LANES), F32),
    )
    out_specs = tuple(
        pl.BlockSpec((DA_HEADS, tm // _VB, _VR, _VB), lambda i: (0, i, 0, 0)) if len(s.shape) == 4
        else row(s.shape[1]) for s in out_shape)
    return pl.pallas_call(
        _inproj_kernel,
        grid=(T // tm,),
        in_specs=[row(D_MODEL), full(norm1_g), full(w_main), full(w_vt), full(w_gates)],
        out_specs=out_specs,
        out_shape=out_shape,
        compiler_params=pltpu.CompilerParams(dimension_semantics=("arbitrary",), vmem_limit_bytes=VMEM_LIMIT),
        name="in_proj",
    )(x, norm1_g, w_main, w_vt, w_gates)


def _lam_from(lamv_ref, lam_init):
    lv = lamv_ref[...]
    s1 = jnp.sum(lv[0:1, :] * lv[1:2, :], axis=-1, keepdims=True)
    s2 = jnp.sum(lv[2:3, :] * lv[3:4, :], axis=-1, keepdims=True)
    return jnp.exp(s1) - jnp.exp(s2) + lam_init


def _stack_maps(q):
    lane = lax.broadcasted_iota(jnp.int32, q.shape, 1)
    zero = jnp.zeros_like(q)
    return jnp.concatenate([jnp.where(lane < DA_HEAD_DIM, q, zero), jnp.where(lane >= DA_HEAD_DIM, q, zero)], axis=0)


def _attn_finish(acc, l, lam, subg, lam_init, tq):
    o = acc[:tq] / l[:tq] - lam * (acc[tq:] / l[tq:])
    return _rms(o, subg) * (1.0 - lam_init)


_CT = 256
_QK_SPLIT = 4


def _attn_prompt_kernel(lamv_ref, subg_ref, q_ref, k_ref, vt_ref, o_ref, qs_sc, m_sc, acc_sc, s_sc, pb_sc,
                        al_sc, *, tq, tk, lam_init):
    i = pl.program_id(2)
    nq = 2 * tq
    nct = nq // _CT
    nvt = tk // _VB
    kpq = tq // tk
    qt = jnp.transpose(q_ref[...].astype(F32))
    d = lax.broadcasted_iota(jnp.int32, qt.shape, 0)
    qs_sc[:, 0:tq] = jnp.where(d < DA_HEAD_DIM, qt, 0.0).astype(BF16)
    qs_sc[:, tq:nq] = jnp.where(d >= DA_HEAD_DIM, qt, 0.0).astype(BF16)
    m_sc[...] = jnp.full(m_sc.shape, NEG_INF, F32)
    acc_sc[...] = jnp.zeros(acc_sc.shape, F32)

    def kblock(j):
        return k_ref[pl.ds(pl.multiple_of(j * tk, tk), tk), :]

    def qk(kb, c):
        q = qs_sc[:, c * _CT:(c + 1) * _CT]
        h = tk // _QK_SPLIT
        return jnp.concatenate([_dot(kb[r * h:(r + 1) * h], q) for r in range(_QK_SPLIT)], axis=0)

    def soft(c, s, rel):
        cs = slice(c * _CT, (c + 1) * _CT)
        if rel is not None:
            key = lax.broadcasted_iota(jnp.int32, s.shape, 0) + rel
            qry = lax.broadcasted_iota(jnp.int32, s.shape, 1) + ((c * _CT) % tq)
            s = jnp.where((qry // CHUNK) >= (key // CHUNK), s, NEG_INF)
        m_prev = m_sc[:, cs]
        m_new = jnp.maximum(m_prev, jnp.max(s, axis=0, keepdims=True))
        alpha = jnp.exp2(m_prev - m_new)
        p = jnp.exp2(s - m_new)
        m_sc[:, cs] = m_new
        return p.astype(BF16), alpha

    def pv(j, c, pb, alpha):
        cs = slice(c * _CT, (c + 1) * _CT)
        upd = _dot(vt_ref[j * nvt], pb[0:_VB, :])
        for t in range(1, nvt):
            upd = upd + _dot(vt_ref[j * nvt + t], pb[t * _VB:(t + 1) * _VB, :])
        acc_sc[:, cs] = alpha * acc_sc[:, cs] + upd

    s_sc[...] = qk(kblock(0), 0)
    pb_sc[...] = jnp.zeros(pb_sc.shape, BF16)
    al_sc[...] = jnp.ones(al_sc.shape, F32)

    def run(steps, j_prev, c_prev, nxt):
        s_cur = s_sc[...]
        pend = None
        for n, (j, c, rel) in enumerate(steps):
            if n + 1 < len(steps):
                s_nxt = qk(kblock(steps[n + 1][0]), steps[n + 1][1])
            elif nxt is not None:
                s_sc[...] = qk(kblock(nxt[0]), nxt[1])
            pb, alpha = soft(c, s_cur, rel)
            if n == 0:
                pv(j_prev, c_prev, pb_sc[...], al_sc[...])
            else:
                pv(steps[n - 1][0], steps[n - 1][1], *pend)
            pend = (pb, alpha)
            if n + 1 < len(steps):
                s_cur = s_nxt
        pb_sc[...] = pend[0]
        al_sc[...] = pend[1]

    def body(j, carry):
        run([(j, c, None) for c in range(nct)], jnp.maximum(j - 1, 0), nct - 1, (j + 1, 0))
        return carry

    j0 = i * kpq
    lax.fori_loop(0, j0, body, 0)
    diag = []
    for r in range(kpq):
        for c in range(nct):
            q_lo = (c * _CT) % tq
            if q_lo + _CT <= r * tk:
                continue
            diag.append((j0 + r, c, None if q_lo >= (r + 1) * tk else r * tk))
    run(diag, jnp.maximum(j0 - 1, 0), nct - 1, None)
    pv(diag[-1][0], diag[-1][1], pb_sc[...], al_sc[...])
    lam = _lam_from(lamv_ref, lam_init)
    acc = acc_sc[0:DA_V_DIM, :]
    l = acc_sc[DA_V_DIM:DA_V_DIM + 1, :]
    ot = acc[:, 0:tq] / l[:, 0:tq] - lam * (acc[:, tq:nq] / l[:, tq:nq])
    ot = ot * lax.rsqrt(jnp.mean(ot * ot, axis=0, keepdims=True) + RMS_EPS) * subg_ref[...] * (1.0 - lam_init)
    o_ref[...] = jnp.transpose(ot).astype(o_ref.dtype)


def _attn_prompt(qb, kb, vt, lamv, subg_col, lam_init, tq, tk):
    B, L, _ = qb.shape
    assert tk % _VB == 0 and (2 * tq) % _CT == 0 and tq % tk == 0 and L % tq == 0 and _CT <= tk
    kern = functools.partial(_attn_prompt_kernel, tq=tq, tk=tk, lam_init=lam_init)
    nvb = L // _VB
    return pl.pallas_call(
        kern,
        grid=(B, DA_HEADS, L // tq),
        in_specs=[
            pl.BlockSpec(lamv.shape, lambda b, h, i: (0, 0)),
            pl.BlockSpec(subg_col.shape, lambda b, h, i: (0, 0)),
            pl.BlockSpec((None, tq, LANES), lambda b, h, i: (b, i, h)),
            pl.BlockSpec((None, L, LANES), lambda b, h, i: (b, 0, h)),
            pl.BlockSpec((None, nvb, _VR, _VB), lambda b, h, i: (h, b, 0, 0)),
        ],
        out_specs=pl.BlockSpec((None, tq, LANES), lambda b, h, i: (b, i, h)),
        out_shape=jax.ShapeDtypeStruct((B, L, DA_V), BF16),
        scratch_shapes=[
            pltpu.VMEM((LANES, 2 * tq), BF16),
            pltpu.VMEM((1, 2 * tq), F32),
            pltpu.VMEM((_VR, 2 * tq), F32),
            pltpu.VMEM((tk, _CT), F32),
            pltpu.VMEM((tk, _CT), BF16),
            pltpu.VMEM((1, _CT), F32),
        ],
        compiler_params=pltpu.CompilerParams(
            dimension_semantics=("arbitrary", "arbitrary", "arbitrary"), vmem_limit_bytes=VMEM_LIMIT),
        name="attn_prompt",
    )(lamv, subg_col, qb, kb, vt)


def _attn_sample_kernel(lamv_ref, subg_ref, q_ref, kp_ref, vp_ref, kn_ref, vn_ref, o_ref, *, lam_init):
    tq = q_ref.shape[0]
    qs = _stack_maps(q_ref[...])
    sp = _dot(qs, kp_ref[...].astype(BF16), _NT)
    sn = _dot(qs, kn_ref[...], _NT)
    m = jnp.maximum(jnp.max(sp, axis=1, keepdims=True), jnp.max(sn, axis=1, keepdims=True))
    pp = jnp.exp(sp - m)
    pn = jnp.exp(sn - m)
    l = jnp.sum(pp, axis=1, keepdims=True) + jnp.sum(pn, axis=1, keepdims=True)
    acc = _dot(pp.astype(BF16), vp_ref[...].astype(BF16)) + _dot(pn.astype(BF16), vn_ref[...])
    lam = _lam_from(lamv_ref, lam_init)
    o_ref[...] = _attn_finish(acc, l, lam, subg_ref[...], lam_init, tq).astype(o_ref.dtype)


def _attn_sample(qb, kb, vb, cache_k, cache_v, lamv, subg, lam_init):
    B, L, _ = qb.shape
    P = cache_k.shape[1]
    kern = functools.partial(_attn_sample_kernel, lam_init=lam_init)
    new = pl.BlockSpec((None, L, LANES), lambda b, h: (b, 0, h))
    past = pl.BlockSpec((None, P, LANES), lambda b, h: (b, 0, h))
    return pl.pallas_call(
        kern,
        grid=(B, DA_HEADS),
        in_specs=[
            pl.BlockSpec(lamv.shape, lambda b, h: (0, 0)),
            pl.BlockSpec(subg.shape, lambda b, h: (0, 0)),
            new, past, past, new, new,
        ],
        out_specs=new,
        out_shape=jax.ShapeDtypeStruct((B, L, DA_V), BF16),
        compiler_params=pltpu.CompilerParams(dimension_semantics=("arbitrary", "arbitrary")),
        name="attn_sample",
    )(lamv, subg, qb, cache_k, cache_v, kb, vb)


_PAD = 8


def _gdn_kernel(u_ref, gt_ref, z_ref, c0_ref, s0_ref, cw_ref, alog_ref, dtb_ref, gn_ref,
                og_ref, sout_ref, ubuf, s_sc, *, nchunks):
    c = pl.program_id(1)
    nc = pl.num_programs(1)
    C = CHUNK
    R = C * nchunks

    @pl.when(c == 0)
    def _():
        ubuf[0:_PAD, :] = c0_ref[...]
        s_sc[...] = s0_ref[...]

    u = u_ref[...]
    ubuf[_PAD:_PAD + R, :] = u
    cw = cw_ref[...]
    y = u * cw[3:4, :]
    for j in range(1, GDN_CONV):
        y = y + ubuf[_PAD - j:_PAD - j + R, :] * cw[3 - j:4 - j, :]
    ubuf[0:_PAD, :] = u[R - _PAD:, :]
    cq = y * jax.nn.sigmoid(y)

    gt = gt_ref[...]
    beta_all = jax.nn.sigmoid(gt)
    xg = gt + dtb_ref[...]
    softplus = jnp.maximum(xg, 0.0) + jnp.log1p(jnp.exp(-jnp.abs(xg)))
    g_all = -jnp.exp(alog_ref[...]) * softplus
    row = lax.broadcasted_iota(jnp.int32, (C, C), 0)
    col = lax.broadcasted_iota(jnp.int32, (C, C), 1)
    causal = row >= col
    strict = row > col
    eye = jnp.where(row == col, 1.0, 0.0).astype(F32)
    rr = lax.broadcasted_iota(jnp.int32, (R, R), 0)
    cc = lax.broadcasted_iota(jnp.int32, (R, R), 1)
    trilb = jnp.where(jnp.logical_and(rr >= cc, (rr // C) == (cc // C)), 1.0, 0.0).astype(BF16)
    ghi = g_all.astype(BF16)
    r1 = g_all - ghi.astype(F32)
    gmid = r1.astype(BF16)
    glo = (r1 - gmid.astype(F32)).astype(BF16)
    G_all = _dot(trilb, ghi) + (_dot(trilb, gmid) + _dot(trilb, glo))
    G_t = jnp.transpose(G_all)
    gn = gn_ref[...]

    qn, kn = [], []
    for h in range(GDN_HEADS):
        q = cq[:, h * GDN_HEAD_DIM:(h + 1) * GDN_HEAD_DIM]
        k = cq[:, GDN_K + h * GDN_HEAD_DIM:GDN_K + (h + 1) * GDN_HEAD_DIM]
        qn.append(q * lax.rsqrt(jnp.sum(q * q, axis=-1, keepdims=True) + 1e-6) * (GDN_HEAD_DIM ** -0.5))
        kn.append(k * lax.rsqrt(jnp.sum(k * k, axis=-1, keepdims=True) + 1e-6))

    keys = [(n, h) for n in range(nchunks) for h in range(GDN_HEADS)]
    st = {}
    for (n, h) in keys:
        rs = slice(n * C, (n + 1) * C)
        q = qn[h][rs]
        k = kn[h][rs]
        v = cq[rs, 2 * GDN_K + h * GDN_HEAD_DIM:2 * GDN_K + (h + 1) * GDN_HEAD_DIM]
        beta = beta_all[rs, h:h + 1]
        Gc = G_all[rs, GDN_HEADS + h:GDN_HEADS + h + 1]
        Gr = G_t[GDN_HEADS + h:GDN_HEADS + h + 1, rs]
        Gl = Gc[C - 1:C, :]
        decay = jnp.where(causal, jnp.exp(jnp.where(causal, Gc - Gr, 0.0)), 0.0)
        eG = jnp.exp(Gc)
        kb = k * beta
        st[n, h] = dict(decay=decay, kbf=k.astype(BF16), kkb=kb.astype(BF16), qb=q.astype(BF16),
                        rhs=jnp.concatenate([v * beta, kb * eG], axis=1),
                        qg=(q * eG).astype(BF16), kg=k * jnp.exp(Gl - Gc), gl=jnp.exp(Gl))
    for key in keys:
        d = st[key]
        d["kk"] = _dot(d["kkb"], d["kbf"], _NT)
    for key in keys:
        d = st[key]
        d["qk"] = (_dot(d["qb"], d["kbf"], _NT) * d["decay"]).astype(BF16)
    for key in keys:
        d = st[key]
        nm = jnp.where(strict, d["kk"] * d["decay"], 0.0)
        d["inv"] = eye - nm
        d["pw"] = nm.astype(BF16)
    for _ in range(5):
        for key in keys:
            d = st[key]
            d["pwf"] = _dot(d["pw"], d["pw"])
        for key in keys:
            d = st[key]
            d["pw"] = d["pwf"].astype(BF16)
            d["inv"] = d["inv"] + _dot(d["inv"].astype(BF16), d["pw"])
    for key in keys:
        d = st[key]
        invb = d["inv"].astype(BF16)
        rh, rl = _split_bf16(d["rhs"])
        sol = _dot(invb, rh) + _dot(invb, rl)
        d["u"] = sol[:, :GDN_HEAD_DIM]
        d["w"] = sol[:, GDN_HEAD_DIM:].astype(BF16)
        d["kgt"] = jnp.transpose(d["kg"]).astype(BF16)

    S = [s_sc[h] for h in range(GDN_HEADS)]
    heads = range(GDN_HEADS)
    for n in range(nchunks):
        rs = slice(n * C, (n + 1) * C)
        Sb = [S[h].astype(BF16) for h in heads]
        ws = [_dot(st[n, h]["w"], Sb[h]) for h in heads]
        qs = [_dot(st[n, h]["qg"], Sb[h]) for h in heads]
        vb = [(st[n, h]["u"] - ws[h]).astype(BF16) for h in heads]
        o = [qs[h] + _dot(st[n, h]["qk"], vb[h]) for h in heads]
        for h in heads:
            S[h] = S[h] * st[n, h]["gl"] + _dot(st[n, h]["kgt"], vb[h])
        for h in heads:
            sl = slice(h * GDN_HEAD_DIM, (h + 1) * GDN_HEAD_DIM)
            zz = z_ref[rs, sl]
            og_ref[rs, sl] = (_rms(o[h], gn) * (zz * jax.nn.sigmoid(zz))).astype(og_ref.dtype)
    for h in heads:
        s_sc[h] = S[h]

    @pl.when(c == nc - 1)
    def _():
        sout_ref[...] = s_sc[...]


def _gdn(conv_in, gates, z, conv0p, s0, conv_w, alog_p, dtb_p, gn, nchunks):
    B, L, _ = conv_in.shape
    R = CHUNK * nchunks
    assert L % R == 0
    tok = lambda n: pl.BlockSpec((None, R, n), lambda b, c: (b, c, 0))
    full = lambda a: pl.BlockSpec(a.shape, lambda b, c: (0,) * a.ndim)
    return pl.pallas_call(
        functools.partial(_gdn_kernel, nchunks=nchunks),
        grid=(B, L // R),
        in_specs=[
            tok(GDN_CONV_CH), tok(LANES), tok(GDN_V),
            pl.BlockSpec((None, _PAD, GDN_CONV_CH), lambda b, c: (b, 0, 0)),
            pl.BlockSpec((None, GDN_HEADS, GDN_HEAD_DIM, GDN_HEAD_DIM), lambda b, c: (b, 0, 0, 0)),
            full(conv_w), full(alog_p), full(dtb_p), full(gn),
        ],
        out_specs=(
            tok(GDN_V),
            pl.BlockSpec((None, GDN_HEADS, GDN_HEAD_DIM, GDN_HEAD_DIM), lambda b, c: (b, 0, 0, 0)),
        ),
        out_shape=(
            jax.ShapeDtypeStruct((B, L, GDN_V), BF16),
            jax.ShapeDtypeStruct((B, GDN_HEADS, GDN_HEAD_DIM, GDN_HEAD_DIM), F32),
        ),
        scratch_shapes=[
            pltpu.VMEM((_PAD + R, GDN_CONV_CH), F32),
            pltpu.VMEM((GDN_HEADS, GDN_HEAD_DIM, GDN_HEAD_DIM), F32),
        ],
        compiler_params=pltpu.CompilerParams(dimension_semantics=("arbitrary", "arbitrary"),
                                             vmem_limit_bytes=VMEM_LIMIT),
        name="gdn",
    )(conv_in, gates, z, conv0p, s0, conv_w, alog_p, dtb_p, gn)


def _postmix_kernel(x_ref, oa_ref, og_ref, wo_ref, g2_ref, wr_ref, br_ref, x1_ref, h2_ref, comb_ref, cnt_ref, cnt_sc):
    i = pl.program_id(0)

    @pl.when(i == 0)
    def _():
        cnt_sc[...] = jnp.zeros(cnt_sc.shape, F32)

    mix = _dot(oa_ref[...], wo_ref[0:DA_V, :]) + _dot(og_ref[...], wo_ref[DA_V:, :])
    x1 = x_ref[...] + mix
    x1_ref[...] = x1
    h2 = _rms(x1, g2_ref[...])
    h2_ref[...] = _pack_bf16_pairs(h2)
    logits = _dot3(h2, wr_ref[...]) + br_ref[...]
    lane = lax.broadcasted_iota(jnp.int32, logits.shape, 1).astype(F32)
    work = logits
    sel = jnp.zeros(logits.shape, jnp.bool_)
    top = jnp.max(work, axis=1, keepdims=True)
    for _ in range(TOP_K):
        mx = jnp.max(work, axis=1, keepdims=True)
        first = jnp.min(jnp.where(work == mx, lane, float(LANES)), axis=1, keepdims=True)
        pick = lane == first
        sel = jnp.logical_or(sel, pick)
        work = jnp.where(pick, -jnp.inf, work)
    e = jnp.where(sel, jnp.exp(logits - top), 0.0)
    flags = pltpu.roll(jnp.where(sel, 1.0, 0.0), N_EXPERTS, axis=1)
    comb_ref[...] = e / jnp.sum(e, axis=1, keepdims=True) + flags
    cnt_sc[...] = cnt_sc[...] + jnp.sum(flags, axis=0, keepdims=True)
    cnt_ref[...] = cnt_sc[...]


def _post_mix(x, oa, og, wo, g2, wr, br, tm):
    T = x.shape[0]
    row = lambda n: pl.BlockSpec((tm, n), lambda i: (i, 0))
    full = lambda a: pl.BlockSpec(a.shape, lambda i: (0,) * a.ndim)
    return pl.pallas_call(
        _postmix_kernel,
        grid=(T // tm,),
        in_specs=[row(D_MODEL), row(DA_V), row(GDN_V), full(wo), full(g2), full(wr), full(br)],
        out_specs=(row(D_MODEL), row(D_MODEL // 2), row(LANES), pl.BlockSpec((1, LANES), lambda i: (0, 0))),
        out_shape=(
            jax.ShapeDtypeStruct((T, D_MODEL), F32),
            jax.ShapeDtypeStruct((T, D_MODEL // 2), jnp.uint32),
            jax.ShapeDtypeStruct((T, LANES), F32),
            jax.ShapeDtypeStruct((1, LANES), F32),
        ),
        scratch_shapes=[pltpu.VMEM((1, LANES), F32)],
        compiler_params=pltpu.CompilerParams(dimension_semantics=("arbitrary",), vmem_limit_bytes=VMEM_LIMIT),
        name="post_mix",
    )(x, oa, og, wo, g2, wr, br)


_RT = 512


def _route_kernel(comb_ref, off_ref, slot_ref, gk_ref, carry_sc):
    i = pl.program_id(0)

    @pl.when(i == 0)
    def _():
        carry_sc[...] = jnp.zeros(carry_sc.shape, F32)

    comb = comb_ref[...]
    tr = comb.shape[0]
    lane = lax.broadcasted_iota(jnp.int32, comb.shape, 1).astype(F32)
    sel = jnp.where(lane >= float(N_EXPERTS), comb, 0.0)
    r = lax.broadcasted_iota(jnp.int32, (tr, tr), 0)
    c = lax.broadcasted_iota(jnp.int32, (tr, tr), 1)
    below = jnp.where(r > c, 1.0, 0.0).astype(BF16)
    pos = _dot(below, sel.astype(BF16)) + (carry_sc[...] + off_ref[...])
    carry_sc[...] = carry_sc[...] + jnp.sum(sel, axis=0, keepdims=True)
    rest = sel
    slot = jnp.zeros(comb.shape, F32)
    gk = jnp.zeros(comb.shape, F32)
    for k in range(TOP_K):
        first = jnp.min(jnp.where(rest > 0.0, lane, float(LANES)), axis=1, keepdims=True)
        hit = lane == first
        expert = first - float(N_EXPERTS)
        slot = jnp.where(lane == float(k), jnp.sum(jnp.where(hit, pos, 0.0), axis=1, keepdims=True), slot)
        gk = jnp.where(lane == float(k), jnp.sum(jnp.where(lane == expert, comb, 0.0), axis=1, keepdims=True), gk)
        rest = jnp.where(hit, 0.0, rest)
    slot_ref[...] = slot.astype(jnp.int32)
    gk_ref[...] = gk


def _route(comb, off_lanes):
    T = comb.shape[0]
    tr = _pick(T, (_RT, 256, 128, 64))
    row = pl.BlockSpec((tr, LANES), lambda i: (i, 0))
    return pl.pallas_call(
        _route_kernel,
        grid=(T // tr,),
        in_specs=[row, pl.BlockSpec((1, LANES), lambda i: (0, 0))],
        out_specs=(row, row),
        out_shape=(
            jax.ShapeDtypeStruct((T, LANES), jnp.int32),
            jax.ShapeDtypeStruct((T, LANES), F32),
        ),
        scratch_shapes=[pltpu.VMEM((1, LANES), F32)],
        compiler_params=pltpu.CompilerParams(dimension_semantics=("arbitrary",)),
        name="route",
    )(comb, off_lanes)


_SC_WIN = 128
_SC_COLS = 128


def _sc_mesh():
    return plsc.VectorSubcoreMesh(core_axis_name="c", subcore_axis_name="s")


def _scatter_rows(x, idx, n_out):
    K, T = idx.shape
    D = x.shape[1]
    assert T % _SC_WIN == 0 and D % _SC_COLS == 0
    flat = idx.reshape(1, K * T)
    nt = T // _SC_WIN

    @pl.kernel(out_type=jax.ShapeDtypeStruct((n_out, D), x.dtype), mesh=_sc_mesh(), scratch_types=[])
    def kern(x_hbm, i_hbm, o_hbm):
        for dc in range(D // _SC_COLS):
            def body(x_vmem, i_vmem, dc=dc):
                pltpu.sync_copy(x_vmem, o_hbm.at[i_vmem.at[0], pl.ds(dc * _SC_COLS, _SC_COLS)])

            pltpu.emit_pipeline(
                body,
                grid=(K * nt,),
                in_specs=[pl.BlockSpec((_SC_WIN, _SC_COLS), lambda i, dc=dc: (i % nt, dc)),
                          pl.BlockSpec((1, _SC_WIN), lambda i: (0, i))],
                out_specs=[],
                core_axis_name=("c", "s"),
                dimension_semantics=(pltpu.PARALLEL,),
            )(x_hbm, i_hbm)

    return kern(x, flat)


def _gather_rows(y, idx):
    K, T = idx.shape
    D = y.shape[1]
    assert (K * T) % _SC_WIN == 0 and D % _SC_COLS == 0
    flat = idx.reshape(1, K * T)

    @pl.kernel(out_type=jax.ShapeDtypeStruct((K * T, D), y.dtype), mesh=_sc_mesh(), scratch_types=[])
    def kern(y_hbm, i_hbm, o_hbm):
        for dc in range(D // _SC_COLS):
            def body(i_vmem, o_vmem, dc=dc):
                pltpu.sync_copy(y_hbm.at[i_vmem.at[0], pl.ds(dc * _SC_COLS, _SC_COLS)], o_vmem)

            pltpu.emit_pipeline(
                body,
                grid=(K * T // _SC_WIN,),
                in_specs=[pl.BlockSpec((1, _SC_WIN), lambda i: (0, i))],
                out_specs=[pl.BlockSpec((_SC_WIN, _SC_COLS), lambda i, dc=dc: (i, dc))],
                core_axis_name=("c", "s"),
                dimension_semantics=(pltpu.PARALLEL,),
            )(i_hbm, o_hbm)

    return kern(y, flat)


_FF_BLK = 512


def _expert_kernel(te_ref, na_ref, x_ref, wg_ref, bg_ref, wu_ref, bu_ref, wd_ref, bd_ref, y_ref, *wb_refs, cast):
    n = pl.program_id(0)
    if cast:
        wgb, wub, wdb = wb_refs

        @pl.when(jnp.logical_or(n == 0, te_ref[n] != te_ref[jnp.maximum(n - 1, 0)]))
        def _():
            wgb[...] = wg_ref[...].astype(BF16)
            wub[...] = wu_ref[...].astype(BF16)
            wdb[...] = wd_ref[...].astype(BF16)
    else:
        wgb, wub, wdb = wg_ref, wu_ref, wd_ref

    @pl.when(n < na_ref[0])
    def _():
        h = _unpack_bf16_pairs(x_ref[...]).astype(BF16)
        o = bd_ref[...]
        for f in range(D_FF // _FF_BLK):
            fs = slice(f * _FF_BLK, (f + 1) * _FF_BLK)
            gt = jnp.minimum(_dot(h, wgb[:, fs]) + bg_ref[:, fs], SWIGLU_LIMIT)
            up = jnp.clip(_dot(h, wub[:, fs]) + bu_ref[:, fs], -SWIGLU_LIMIT, SWIGLU_LIMIT)
            act = (up + 1.0) * gt * jax.nn.sigmoid(SWIGLU_ALPHA * gt)
            o = o + _dot(act.astype(BF16), wdb[fs, :])
        y_ref[...] = _pack_bf16_pairs(o)


def _experts(xs, tile_expert, n_active, wg, bg, wu, bu, wd, bd, tmr):
    S = xs.shape[0]
    cast = wg.dtype != BF16
    rows = pl.BlockSpec((tmr, D_MODEL // 2), lambda n, te, na: (jnp.minimum(n, na[0] - 1), 0))
    wspec = pl.BlockSpec((None, D_MODEL, D_FF), lambda n, te, na: (te[n], 0, 0))
    bspec = pl.BlockSpec((None, 1, D_FF), lambda n, te, na: (te[n], 0, 0))
    y_shape = jax.ShapeDtypeStruct((S, D_MODEL // 2), jnp.uint32)
    wb_shape = jax.ShapeDtypeStruct((N_EXPERTS, D_MODEL, D_FF), BF16)
    out = pl.pallas_call(
        functools.partial(_expert_kernel, cast=cast),
        grid_spec=pltpu.PrefetchScalarGridSpec(
            num_scalar_prefetch=2,
            grid=(S // tmr,),
            in_specs=[rows, wspec, bspec, wspec, bspec, wspec, bspec],
            out_specs=(rows, wspec, wspec, wspec) if cast else rows,
        ),
        out_shape=(y_shape, wb_shape, wb_shape, wb_shape) if cast else y_shape,
        compiler_params=pltpu.CompilerParams(dimension_semantics=("arbitrary",), vmem_limit_bytes=VMEM_LIMIT),
        name="experts",
    )(tile_expert, n_active, xs, wg, bg, wu, bu, wd, bd)
    return (out[0], out[1:]) if cast else (out, (wg, wu, wd))


def _combine_kernel(x1_ref, yg_ref, gk_ref, fg_ref, y_ref):
    gk = gk_ref[...]
    moe = gk[:, 0:1] * _unpack_bf16_pairs(yg_ref[0])
    for k in range(1, TOP_K):
        moe = moe + gk[:, k:k + 1] * _unpack_bf16_pairs(yg_ref[k])
    y_ref[...] = _rms(x1_ref[...] + moe, fg_ref[...])


def _combine(x1, yg, gk, fg, tm):
    T = x1.shape[0]
    row = lambda n: pl.BlockSpec((tm, n), lambda i: (i, 0))
    return pl.pallas_call(
        _combine_kernel,
        grid=(T // tm,),
        in_specs=[row(D_MODEL), pl.BlockSpec((TOP_K, tm, D_MODEL // 2), lambda i: (0, i, 0)), row(LANES),
                  pl.BlockSpec(fg.shape, lambda i: (0, 0))],
        out_specs=row(D_MODEL),
        out_shape=jax.ShapeDtypeStruct((T, D_MODEL), F32),
        compiler_params=pltpu.CompilerParams(dimension_semantics=("arbitrary",), vmem_limit_bytes=VMEM_LIMIT),
        name="combine",
    )(x1, yg, gk, fg)


def _moe(h2, comb, cnt, x1, wg, bg, wu, bu, wd, bd, fg, tmr, after=None):
    T = h2.shape[0]
    cnt_e = cnt[0, N_EXPERTS:2 * N_EXPERTS].astype(jnp.int32)
    tiles_e = jnp.maximum((cnt_e + tmr - 1) // tmr, 1)
    tile_end = jnp.cumsum(tiles_e)
    off = (tile_end - tiles_e) * tmr
    n_tiles = T * TOP_K // tmr + N_EXPERTS
    n_active = tile_end[-1:].astype(jnp.int32)
    tile_id = jnp.minimum(jnp.arange(n_tiles, dtype=jnp.int32), n_active[0] - 1)
    tile_expert = jnp.sum((tile_end[None, :] <= tile_id[:, None]).astype(jnp.int32), axis=1)
    tile_expert = jnp.minimum(tile_expert, N_EXPERTS - 1).astype(jnp.int32)
    off_lanes = jnp.zeros((1, LANES), F32).at[0, N_EXPERTS:2 * N_EXPERTS].set(off.astype(F32))
    slot128, gk = _route(comb, off_lanes)
    slot = slot128[:, :TOP_K].T
    xs = _scatter_rows(h2, slot, n_tiles * tmr)
    ys, wb = _experts(xs, tile_expert, n_active, wg, bg, wu, bu, wd, bd, tmr)
    if after is not None:
        ys, after = lax.optimization_barrier((ys, after))
    yg = _gather_rows(ys, slot).reshape(TOP_K, T, D_MODEL // 2)
    return _combine(x1, yg, gk, fg, _pick(T, (256,))), wb, after


def _pick(n, prefs):
    for t in prefs:
        if n % t == 0:
            return t
    raise ValueError(f"no tile for {n}")


def _layer(x, k_past, v_past, S0, conv0, lam_init, wts):
    (norm1_g, w_main, w_vt, w_gates, lamv, subg, conv_w, alog_p, dtb_p, gn, wo, g2, wr, br,
     wg, bg, wu, bu, wd, bd) = wts
    B, L, _ = x.shape
    T = B * L
    xt = x.reshape(T, D_MODEL)
    tm = _pick(T, (512, 256))
    qb, q2, k, kb, v, vb, vt, conv_in, z, gates = _in_proj(xt, norm1_g, w_main, w_vt, w_gates, tm)
    r3 = lambda a: a.reshape(B, L, a.shape[-1])
    if k_past is None:
        oa = _attn_prompt(r3(q2), r3(kb), vt, lamv, subg.reshape(DA_V_DIM, 1), lam_init,
                          _pick(L, (2048, 1024, 512, 256)), _pick(L, (512, 256)))
    else:
        P = k_past.shape[1]
        oa = _attn_sample(r3(qb), r3(kb), r3(vb), k_past.reshape(B, P, DA_QK), v_past.reshape(B, P, DA_V),
                          lamv, subg, lam_init)
    conv_in3 = r3(conv_in)
    conv0p = jnp.concatenate([jnp.zeros((B, _PAD - (GDN_CONV - 1), GDN_CONV_CH), F32), conv0.astype(F32)], axis=1)
    og, s_new = _gdn(conv_in3, r3(gates), r3(z), conv0p, S0.astype(F32), conv_w, alog_p, dtb_p, gn,
                     _pick(L // CHUNK, (8, 4, 2, 1)))
    x1, h2, comb, cnt = _post_mix(xt, oa.reshape(T, DA_V), og.reshape(T, GDN_V), wo, g2, wr, br, tm)
    return (x1, h2, comb, cnt, k.reshape(B, L, DA_HEADS, 2, DA_HEAD_DIM), v.reshape(B, L, DA_HEADS, DA_V_DIM),
            s_new, conv_in3[:, L - (GDN_CONV - 1):, :])


def kernel(x_prompt, x_sample, cache_k, cache_v, state_ssm, state_conv, norm1_g, w_in, lambda_q1, lambda_k1,
           lambda_q2, lambda_k2, da_subln_g, conv_w, a_log, dt_bias, gdn_norm_g, w_out, norm2_g, w_router,
           b_router, w_gate, b_gate, w_up, b_up, w_down, b_down, final_g):
    depth = w_in.shape[0]
    assert depth == 1, "single-layer trunk"
    l = 0
    lam_init = 0.8 - 0.6 * math.exp(-0.3 * l)
    bp = x_prompt.shape[0]
    pad_lanes = lambda a, fill=0.0: jnp.concatenate(
        [a.astype(F32), jnp.full(a.shape[:-1] + (LANES - a.shape[-1],), fill, F32)], axis=-1)
    w_gates = pad_lanes(w_in[l][:, OFF_B:])
    zeros4 = jnp.zeros((GDN_HEADS,), F32)
    alog_p = pad_lanes(jnp.concatenate([zeros4, a_log[l].astype(F32)])[None, :])
    dtb_p = pad_lanes(jnp.concatenate([zeros4, dt_bias[l].astype(F32)])[None, :])
    wts = (
        norm1_g[l][None, :].astype(F32),
        w_in[l][:, :OFF_B].astype(BF16),
        w_in[l][:, OFF_VA:OFF_CONV].T.astype(BF16),
        w_gates,
        jnp.stack([lambda_q1[l], lambda_k1[l], lambda_q2[l], lambda_k2[l]]).astype(F32),
        da_subln_g[l][None, :].astype(F32),
        conv_w[l].astype(F32),
        alog_p, dtb_p,
        gdn_norm_g[l][None, :].astype(F32),
        w_out[l].astype(BF16),
        norm2_g[l][None, :].astype(F32),
        pad_lanes(w_router[l]),
        pad_lanes(b_router[l][None, :], NEG_INF),
        w_gate[l].astype(F32), b_gate[l][:, None, :].astype(F32),
        w_up[l].astype(F32), b_up[l][:, None, :].astype(F32),
        w_down[l].astype(F32), b_down[l][:, None, :].astype(F32),
    )
    moe_w = wts[14:]
    fg = final_g[None, :].astype(F32)

    s0 = jnp.zeros((bp, GDN_HEADS, GDN_HEAD_DIM, GDN_HEAD_DIM), x_prompt.dtype)
    c0 = jnp.zeros((bp, GDN_CONV - 1, GDN_CONV_CH), x_prompt.dtype)
    x1p, h2p, combp, cntp, k1, v1, s1, c1 = _layer(x_prompt, None, None, s0, c0, lam_init, wts)
    wg, bg, wu, bu, wd, bd = moe_w
    y_p, (wgb, wub, wdb), x_sample_late = _moe(h2p, combp, cntp, x1p, wg, bg, wu, bu, wd, bd, fg, 512,
                                               after=x_sample)
    x1s, h2s, combs, cnts, k2, v2, s2, c2 = _layer(x_sample_late, cache_k[l], cache_v[l], state_ssm[l],
                                                   state_conv[l], lam_init, wts)
    y_s, _, _ = _moe(h2s, combs, cnts, x1s, wgb, bg, wub, bu, wdb, bd, fg, 128)
    st = lambda a: a[None]
    return (y_p.reshape(x_prompt.shape), y_s.reshape(x_sample.shape), st(k1), st(v1), st(s1), st(c1),
            st(k2), st(v2), st(s2), st(c2))
```
